```python
import math
import jax, jax.numpy as jnp
from jax import lax
import numpy as np

D_MODEL = 4096
BATCH = 4
SEQ = 2048
DEPTH = 2
DEC_BATCH = 8
DEC_SEQ = 8
PAST_LEN = 16384
PAGE_SIZE = 128

N_BRANCH = 4
BRANCH_W = D_MODEL // N_BRANCH
N_IN_SLOTS = 12
RET_HEADS = 4
RET_HEAD_DIM = BRANCH_W // RET_HEADS
RET_CHUNK = 128
ROPE_BASE = 10000.0
MOBA_HEAD_DIM = 128
MOBA_HEADS = BRANCH_W // MOBA_HEAD_DIM
MOBA_BLOCK = 256
MOBA_TOPK = 3
MOBA_QBLK = 64
REL_BUCKETS = 32
REL_MAX_DIST = 128
SC_WIDTH = 3
CF_WIDTH = 31
FFN_HIDDEN = -(-8 * D_MODEL // (3 * 256)) * 256
N_MOD = 6
EPS = 1e-6

kernel_name = 'hybrid_gated_branch_decoder_step'


def rms_norm(x, g):
    xf = x.astype(jnp.float32)
    y = xf * lax.rsqrt(jnp.mean(xf * xf, axis=-1, keepdims=True) + EPS)
    return (y * g.astype(jnp.float32)).astype(x.dtype)


def layer_norm(x, g, b):
    xf = x.astype(jnp.float32)
    xc = xf - jnp.mean(xf, axis=-1, keepdims=True)
    y = xc * lax.rsqrt(jnp.mean(xc * xc, axis=-1, keepdims=True) + EPS)
    return y * g.astype(jnp.float32) + b.astype(jnp.float32)


def rotary(x, pos):
    half = x.shape[-1] // 2
    inv = jnp.exp(-math.log(ROPE_BASE) * jnp.arange(half, dtype=jnp.float32) / half)
    ang = pos.astype(jnp.float32)[:, None] * inv[None, :]
    cos = jnp.cos(ang)[None, :, None, :]
    sin = jnp.sin(ang)[None, :, None, :]
    x1, x2 = x[..., :half], x[..., half:]
    return jnp.concatenate([x1 * cos - x2 * sin, x1 * sin + x2 * cos], axis=-1)


def retention_chunk(S, q, k, v, log_g):
    L = q.shape[1]
    idx = jnp.arange(L, dtype=jnp.float32)
    diff = idx[:, None] - idx[None, :]
    causal = diff >= 0
    decay = jnp.where(causal[None], jnp.exp(log_g[:, None, None] * jnp.where(causal, diff, 0.0)[None]), 0.0)
    scores = jnp.einsum('bnhd,bmhd->bhnm', q, k) * decay[None]
    o = jnp.einsum('bhnm,bmhe->bnhe', scores, v)
    q_decay = jnp.exp(log_g[None, :] * (idx[:, None] + 1.0))
    o = o + jnp.einsum('bnhd,bhde->bnhe', q * q_decay[None, :, :, None], S)
    k_decay = jnp.exp(log_g[None, :] * ((L - 1.0) - idx[:, None]))
    S_new = jnp.exp(log_g * L)[None, :, None, None] * S + jnp.einsum('bmhd,bmhe->bhde', k * k_decay[None, :, :, None], v)
    return S_new, o


def retention_branch(q, k, v, g, pos, state0, gn_w):
    B, L, _ = q.shape
    H, Dh = RET_HEADS, RET_HEAD_DIM
    f32 = jnp.float32
    log_g = jnp.log1p(-jnp.exp2(-5.0 - jnp.arange(H, dtype=f32)))
    qh = rotary(q.reshape(B, L, H, Dh).astype(f32), pos)
    kh = rotary(k.reshape(B, L, H, Dh).astype(f32), pos) * (Dh ** -0.5)
    vh = v.reshape(B, L, H, Dh).astype(f32)
    chunk = RET_CHUNK if L % RET_CHUNK == 0 else L
    n = L // chunk
    to_chunks = lambda a: a.reshape(B, n, chunk, H, Dh).transpose(1, 0, 2, 3, 4)

    def step(S, qkv):
        return retention_chunk(S, qkv[0], qkv[1], qkv[2], log_g)

    S_fin, o = lax.scan(step, state0.astype(f32), (to_chunks(qh), to_chunks(kh), to_chunks(vh)))
    o = o.transpose(1, 0, 2, 3, 4).reshape(B, L, H, Dh)
    oc = o - jnp.mean(o, axis=-1, keepdims=True)
    o = oc * lax.rsqrt(jnp.mean(oc * oc, axis=-1, keepdims=True) + 1e-5)
    out = o.reshape(B, L, BRANCH_W) * gn_w.astype(f32) * jax.nn.silu(g.astype(f32))
    return out, S_fin


def t5_bucket(n):
    max_exact = REL_BUCKETS // 2
    nf = jnp.maximum(n, 1).astype(jnp.float32)
    large = max_exact + (jnp.log(nf / max_exact) / math.log(REL_MAX_DIST / max_exact) * (REL_BUCKETS - max_exact)).astype(jnp.int32)
    large = jnp.minimum(large, REL_BUCKETS - 1)
    return jnp.where(n < max_exact, n, large)


def moba_attend(q, qpos, k_blocks, v_blocks, block_mean, rel_bias):
    f32 = jnp.float32
    B, Qc, H, Dh = q.shape
    NB = k_blocks.shape[2]
    n_sel = min(MOBA_TOPK, NB)
    qf = q.astype(f32)
    own = qpos // MOBA_BLOCK
    gate = jnp.einsum('bqhd,bnhd->bqhn', qf, block_mean)
    fully_past = jnp.arange(NB)[None, :] < own[:, None]
    gate = jnp.where(fully_past[None, :, None, :], gate, -jnp.inf)
    _, top = lax.top_k(gate, n_sel)
    own_b = jnp.broadcast_to(own[None, :, None, None], (B, Qc, H, 1)).astype(top.dtype)
    blk = jnp.concatenate([top, own_b], axis=-1)
    bi = jnp.arange(B)[:, None, None, None]
    hi = jnp.arange(H)[None, None, :, None]
    kg = k_blocks[bi, hi, blk].astype(f32)
    vg = v_blocks[bi, hi, blk].astype(f32)
    kpos = blk[..., None] * MOBA_BLOCK + jnp.arange(MOBA_BLOCK)
    qp = qpos[None, :, None, None, None]
    slot = jnp.arange(n_sel + 1)
    sel_ok = slot[None, :] < jnp.minimum(own, n_sel)[:, None]
    is_own = (slot == n_sel)[None, None, None, :, None]
    valid = jnp.where(is_own, kpos <= qp, sel_ok[None, :, None, :, None])
    bucket = t5_bucket(jnp.maximum(qp - kpos, 0))
    bias = rel_bias.astype(f32)[bucket, jnp.arange(H)[None, None, :, None, None]]
    logits = jnp.einsum('bqhd,bqhskd->bqhsk', qf, kg) * (Dh ** -0.5) + bias
    logits = jnp.where(valid, logits, -jnp.inf)
    p = jax.nn.softmax(logits.reshape(B, Qc, H, -1), axis=-1).reshape(logits.shape)
    return jnp.einsum('bqhsk,bqhskd->bqhd', p, vg)


def moba_branch(q, k_all, v_all, qpos, rel_bias):
    B, T, H, Dh = k_all.shape
    L = q.shape[1]
    NB = -(-T // MOBA_BLOCK)
    pad = NB * MOBA_BLOCK - T
    kp = jnp.pad(k_all, ((0, 0), (0, pad), (0, 0), (0, 0))).reshape(B, NB, MOBA_BLOCK, H, Dh)
    vp = jnp.pad(v_all, ((0, 0), (0, pad), (0, 0), (0, 0))).reshape(B, NB, MOBA_BLOCK, H, Dh)
    block_mean = jnp.mean(kp.astype(jnp.float32), axis=2)
    k_blocks = kp.transpose(0, 3, 1, 2, 4)
    v_blocks = vp.transpose(0, 3, 1, 2, 4)
    qblk = MOBA_QBLK if L % MOBA_QBLK == 0 else L
    nq = L // qblk
    qs = q.reshape(B, nq, qblk, H, Dh).transpose(1, 0, 2, 3, 4)
    ps = qpos.reshape(nq, qblk)
    o = lax.map(lambda a: moba_attend(a[0], a[1], k_blocks, v_blocks, block_mean, rel_bias), (qs, ps))
    return o.transpose(1, 0, 2, 3, 4).reshape(B, L, H, Dh)


def causal_dwconv(u, buf, w):
    ext = jnp.concatenate([buf.astype(u.dtype), u], axis=1)
    y = lax.conv_general_dilated(ext, w[:, None, :].astype(u.dtype), window_strides=(1,), padding='VALID',
                                 dimension_numbers=('NWC', 'WIO', 'NWC'), feature_group_count=u.shape[-1])
    return y, ext[:, ext.shape[1] - (w.shape[0] - 1):]


def decoder_layer(x, c, pos, k_past, v_past, ret_state, sc_buf, cf_buf,
                  w_ada, b_ada, norm_g, w_in, w_gate, b_gate, ret_gn_w, sconv_w, cconv_w, cconv_b,
                  cnorm_w, cnorm_b, w_branch, w_o, w_ffn_in, w_ffn_out, rel_bias):
    B, L, D = x.shape
    f32 = jnp.float32
    mod = (jax.nn.silu(c) @ w_ada + b_ada).reshape(B, N_MOD, D)[:, :, None, :]
    shift_m, scale_m, gate_m, shift_f, scale_f, gate_f = (mod[:, i] for i in range(N_MOD))
    h = rms_norm(x, norm_g[0]) * (1 + scale_m) + shift_m
    proj = (h @ w_in).reshape(B, L, N_IN_SLOTS, BRANCH_W)
    o_a, ret_new = retention_branch(proj[:, :, 0], proj[:, :, 1], proj[:, :, 2], proj[:, :, 3], pos, ret_state, ret_gn_w)
    q_b = proj[:, :, 4].reshape(B, L, MOBA_HEADS, MOBA_HEAD_DIM)
    k_b = proj[:, :, 5].reshape(B, L, MOBA_HEADS, MOBA_HEAD_DIM)
    v_b = proj[:, :, 6].reshape(B, L, MOBA_HEADS, MOBA_HEAD_DIM)
    k_all = jnp.concatenate([k_past.astype(k_b.dtype), k_b], axis=1)
    v_all = jnp.concatenate([v_past.astype(v_b.dtype), v_b], axis=1)
    o_b = moba_branch(q_b, k_all, v_all, pos, rel_bias).reshape(B, L, BRANCH_W)
    y_c, sc_new = causal_dwconv(proj[:, :, 8] * proj[:, :, 9], sc_buf, sconv_w)
    o_c = proj[:, :, 7] * y_c
    u_d = proj[:, :, 10] * jax.nn.sigmoid(proj[:, :, 11])
    y_d, cf_new = causal_dwconv(u_d, cf_buf, cconv_w)
    o_d = jax.nn.silu(layer_norm(y_d + cconv_b.astype(y_d.dtype), cnorm_w, cnorm_b))
    branches = jnp.stack([o_a.astype(x.dtype), o_b.astype(x.dtype), o_c, o_d.astype(x.dtype)], axis=2)
    up = jnp.einsum('blnc,ncd->blnd', branches, w_branch)
    gates = jax.nn.sigmoid((h @ w_gate + b_gate).astype(f32)).reshape(B, L, N_BRANCH, D)
    merged = jnp.sum(gates * up.astype(f32), axis=2).astype(x.dtype)
    x = x + gate_m * rms_norm(merged @ w_o, norm_g[1])
    h = rms_norm(x, norm_g[2]) * (1 + scale_f) + shift_f
    a, b = jnp.split(h @ w_ffn_in, 2, axis=-1)
    x = x + gate_f * rms_norm((jax.nn.silu(a) * b) @ w_ffn_out, norm_g[3])
    return x, (k_b, v_b, ret_new, sc_new, cf_new)


def setup_inputs(seed: int = 0) -> dict:
    key = jax.random.key(seed)
    ks = jax.random.split(key, 32)
    f32 = jnp.float32
    D = D_MODEL
    n_pages = PAST_LEN // PAGE_SIZE
    n_pool = (DEC_BATCH * n_pages * 5) // 4
    nrm = lambda k, shape, s: jax.random.normal(k, shape, f32) * s
    return {
        'x_prompt': nrm(ks[0], (BATCH, SEQ, D), 1.0),
        'x_sample': nrm(ks[1], (DEC_BATCH, DEC_SEQ, D), 1.0),
        'c_prompt': nrm(ks[2], (BATCH, D), 1.0),
        'c_sample': nrm(ks[3], (DEC_BATCH, D), 1.0),
        'cache_k': nrm(ks[4], (DEPTH, n_pool, PAGE_SIZE, MOBA_HEADS, MOBA_HEAD_DIM), 1.0),
        'cache_v': nrm(ks[5], (DEPTH, n_pool, PAGE_SIZE, MOBA_HEADS, MOBA_HEAD_DIM), 1.0),
        'state_ret': nrm(ks[6], (DEPTH, DEC_BATCH, RET_HEADS, RET_HEAD_DIM, RET_HEAD_DIM), 0.5),
        'state_sconv': nrm(ks[7], (DEPTH, DEC_BATCH, SC_WIDTH - 1, BRANCH_W), 1.0),
        'state_cconv': nrm(ks[8], (DEPTH, DEC_BATCH, CF_WIDTH - 1, BRANCH_W), 0.5),
        'page_table': jax.random.permutation(ks[9], n_pool)[:DEC_BATCH * n_pages].reshape(DEC_BATCH, n_pages).astype(jnp.int32),
        'w_ada': nrm(ks[10], (DEPTH, D, N_MOD * D), 0.5 * D ** -0.5),
        'b_ada': nrm(ks[11], (DEPTH, N_MOD * D), 0.02),
        'norm_g': 1.0 + nrm(ks[12], (DEPTH, 4, D), 0.02),
        'w_in': nrm(ks[13], (DEPTH, D, N_IN_SLOTS * BRANCH_W), D ** -0.5),
        'w_gate': nrm(ks[14], (DEPTH, D, N_BRANCH * D), D ** -0.5),
        'b_gate': nrm(ks[15], (DEPTH, N_BRANCH * D), 0.02),
        'ret_gn_w': 1.0 + nrm(ks[16], (DEPTH, BRANCH_W), 0.02),
        'sconv_w': nrm(ks[17], (DEPTH, SC_WIDTH, BRANCH_W), SC_WIDTH ** -0.5),
        'cconv_w': nrm(ks[18], (DEPTH, CF_WIDTH, BRANCH_W), CF_WIDTH ** -0.5),
        'cconv_b': nrm(ks[19], (DEPTH, BRANCH_W), 0.02),
        'cnorm_w': 1.0 + nrm(ks[20], (DEPTH, BRANCH_W), 0.02),
        'cnorm_b': nrm(ks[21], (DEPTH, BRANCH_W), 0.02),
        'w_branch': nrm(ks[22], (DEPTH, N_BRANCH, BRANCH_W, D), BRANCH_W ** -0.5),
        'w_o': nrm(ks[23], (DEPTH, D, D), D ** -0.5),
        'w_ffn_in': nrm(ks[24], (DEPTH, D, 2 * FFN_HIDDEN), D ** -0.5),
        'w_ffn_out': nrm(ks[25], (DEPTH, FFN_HIDDEN, D), FFN_HIDDEN ** -0.5),
        'rel_bias': nrm(ks[26], (REL_BUCKETS, MOBA_HEADS), 0.5),
    }


def reference(x_prompt, x_sample, c_prompt, c_sample, cache_k, cache_v, state_ret, state_sconv, state_cconv,
              page_table, w_ada, b_ada, norm_g, w_in, w_gate, b_gate, ret_gn_w, sconv_w, cconv_w, cconv_b,
              cnorm_w, cnorm_b, w_branch, w_o, w_ffn_in, w_ffn_out, rel_bias):
    B, S, _ = x_prompt.shape
    DB, DS, _ = x_sample.shape
    page = cache_k.shape[2]
    past_len = page_table.shape[1] * page
    pos_p = jnp.arange(S, dtype=jnp.int32)
    pos_s = past_len + jnp.arange(DS, dtype=jnp.int32)
    xp, xs = x_prompt, x_sample
    kp_l, vp_l, rp_l, sp_l, cp_l = [], [], [], [], []
    ks_l, vs_l, rs_l, ss_l, cs_l = [], [], [], [], []
    for l in range(DEPTH):
        lw = (w_ada[l], b_ada[l], norm_g[l], w_in[l], w_gate[l], b_gate[l], ret_gn_w[l], sconv_w[l], cconv_w[l],
              cconv_b[l], cnorm_w[l], cnorm_b[l], w_branch[l], w_o[l], w_ffn_in[l], w_ffn_out[l], rel_bias)
        empty_kv = jnp.zeros((B, 0, MOBA_HEADS, MOBA_HEAD_DIM), xp.dtype)
        xp, (kp, vp, rp, sp, cp) = decoder_layer(
            xp, c_prompt, pos_p, empty_kv, empty_kv,
            jnp.zeros((B, RET_HEADS, RET_HEAD_DIM, RET_HEAD_DIM), jnp.float32),
            jnp.zeros((B, SC_WIDTH - 1, BRANCH_W), xp.dtype),
            jnp.zeros((B, CF_WIDTH - 1, BRANCH_W), xp.dtype), *lw)
        k_past = cache_k[l, page_table].reshape(DB, past_len, MOBA_HEADS, MOBA_HEAD_DIM)
        v_past = cache_v[l, page_table].reshape(DB, past_len, MOBA_HEADS, MOBA_HEAD_DIM)
        xs, (k_s, v_s, r_s, s_s, c_s) = decoder_layer(
            xs, c_sample, pos_s, k_past, v_past, state_ret[l], state_sconv[l], state_cconv[l], *lw)
        kp_l.append(kp); vp_l.append(vp); rp_l.append(rp); sp_l.append(sp); cp_l.append(cp)
        ks_l.append(k_s); vs_l.append(v_s); rs_l.append(r_s); ss_l.append(s_s); cs_l.append(c_s)
    y_prompt, y_sample = xp, xs
    new_k_prompt = jnp.stack(kp_l).reshape(DEPTH, B, S // page, page, MOBA_HEADS, MOBA_HEAD_DIM)
    new_v_prompt = jnp.stack(vp_l).reshape(DEPTH, B, S // page, page, MOBA_HEADS, MOBA_HEAD_DIM)
    new_k_sample = jnp.stack(ks_l)
    new_v_sample = jnp.stack(vs_l)
    new_ret_prompt = jnp.stack(rp_l)
    new_ret_sample = jnp.stack(rs_l)
    new_sconv_prompt = jnp.stack(sp_l)
    new_sconv_sample = jnp.stack(ss_l)
    new_cconv_prompt = jnp.stack(cp_l)
    new_cconv_sample = jnp.stack(cs_l)
    return (y_prompt, y_sample, new_k_prompt, new_v_prompt, new_k_sample, new_v_sample, new_ret_prompt, new_ret_sample, new_sconv_prompt, new_sconv_sample, new_cconv_prompt, new_cconv_sample)
```

```python
import functools
import math

import jax
import jax.numpy as jnp
from jax import lax
from jax.experimental import pallas as pl
from jax.experimental.pallas import tpu as pltpu

F32 = jnp.float32
BF16 = jnp.bfloat16

N_BRANCH = 4
N_IN_SLOTS = 12
RET_HEADS = 4
RET_CHUNK = 128
ROPE_BASE = 10000.0
MOBA_HEAD_DIM = 128
MOBA_BLOCK = 256
MOBA_TOPK = 3
REL_BUCKETS = 32
REL_MAX_DIST = 128
N_MOD = 6
EPS = 1e-6

VMEM_LIMIT_BYTES = 56 * 1024 * 1024


def _mm_kernel(x_ref, w_ref, o_ref, *, nk):
    part = jnp.dot(x_ref[...].astype(BF16), w_ref[...].astype(BF16), preferred_element_type=F32)
    if nk == 1:
        o_ref[...] = part.astype(o_ref.dtype)
    else:
        k = pl.program_id(2)

        @pl.when(k == 0)
        def _():
            o_ref[...] = part

        @pl.when(k > 0)
        def _():
            o_ref[...] += part


def _pick(dim, prefs):
    for p in prefs:
        if dim % p == 0:
            return p
    return dim


def matmul(x, w, *, tm=None, tn=None, tk=None):
    M, K = x.shape
    _, N = w.shape
    tm = tm or _pick(M, (1024, 512, 256, 128, 64))
    tn = tn or _pick(N, (1024, 512, 256, 128))
    tk = tk or K
    nk = K // tk
    return pl.pallas_call(
        functools.partial(_mm_kernel, nk=nk),
        grid=(M // tm, N // tn, nk),
        in_specs=[pl.BlockSpec((tm, tk), lambda i, j, k: (i, k)),
                  pl.BlockSpec((tk, tn), lambda i, j, k: (k, j))],
        out_specs=pl.BlockSpec((tm, tn), lambda i, j, k: (i, j)),
        out_shape=jax.ShapeDtypeStruct((M, N), F32),
        compiler_params=pltpu.CompilerParams(
            dimension_semantics=("parallel", "parallel", "arbitrary"),
            vmem_limit_bytes=VMEM_LIMIT_BYTES),
        name="matmul",
    )(x, w)


def rms_norm(x, g):
    xf = x.astype(F32)
    y = xf * lax.rsqrt(jnp.mean(xf * xf, axis=-1, keepdims=True) + EPS)
    return y * g.astype(F32)


def layer_norm(x, g, b):
    xf = x.astype(F32)
    xc = xf - jnp.mean(xf, axis=-1, keepdims=True)
    y = xc * lax.rsqrt(jnp.mean(xc * xc, axis=-1, keepdims=True) + EPS)
    return y * g.astype(F32) + b.astype(F32)


def rotary(x, pos):
    half = x.shape[-1] // 2
    inv = jnp.exp(-math.log(ROPE_BASE) * jnp.arange(half, dtype=F32) / half)
    ang = pos.astype(F32)[:, None] * inv[None, :]
    cos = jnp.cos(ang)[None, :, None, :]
    sin = jnp.sin(ang)[None, :, None, :]
    x1, x2 = x[..., :half], x[..., half:]
    return jnp.concatenate([x1 * cos - x2 * sin, x1 * sin + x2 * cos], axis=-1)


def retention_chunk(S, q, k, v, log_g):
    L = q.shape[1]
    idx = jnp.arange(L, dtype=F32)
    diff = idx[:, None] - idx[None, :]
    causal = diff >= 0
    decay = jnp.where(causal[None], jnp.exp(log_g[:, None, None] * jnp.where(causal, diff, 0.0)[None]), 0.0)
    scores = jnp.einsum('bnhd,bmhd->bhnm', q, k) * decay[None]
    o = jnp.einsum('bhnm,bmhe->bnhe', scores, v)
    q_decay = jnp.exp(log_g[None, :] * (idx[:, None] + 1.0))
    o = o + jnp.einsum('bnhd,bhde->bnhe', q * q_decay[None, :, :, None], S)
    k_decay = jnp.exp(log_g[None, :] * ((L - 1.0) - idx[:, None]))
    S_new = jnp.exp(log_g * L)[None, :, None, None] * S + jnp.einsum('bmhd,bmhe->bhde', k * k_decay[None, :, :, None], v)
    return S_new, o


def retention_branch(q, k, v, g, pos, state0, gn_w):
    B, L, W = q.shape
    H = RET_HEADS
    Dh = W // H
    log_g = jnp.log1p(-jnp.exp2(-5.0 - jnp.arange(H, dtype=F32)))
    qh = rotary(q.reshape(B, L, H, Dh), pos)
    kh = rotary(k.reshape(B, L, H, Dh), pos) * (Dh ** -0.5)
    vh = v.reshape(B, L, H, Dh)
    chunk = RET_CHUNK if L % RET_CHUNK == 0 else L
    n = L // chunk
    to_chunks = lambda a: a.reshape(B, n, chunk, H, Dh).transpose(1, 0, 2, 3, 4)

    def step(S, qkv):
        return retention_chunk(S, qkv[0], qkv[1], qkv[2], log_g)

    S_fin, o = lax.scan(step, state0.astype(F32), (to_chunks(qh), to_chunks(kh), to_chunks(vh)))
    o = o.transpose(1, 0, 2, 3, 4).reshape(B, L, H, Dh)
    oc = o - jnp.mean(o, axis=-1, keepdims=True)
    o = oc * lax.rsqrt(jnp.mean(oc * oc, axis=-1, keepdims=True) + 1e-5)
    out = o.reshape(B, L, W) * gn_w.astype(F32) * jax.nn.silu(g)
    return out, S_fin


def t5_bucket(n):
    max_exact = REL_BUCKETS // 2
    nf = jnp.maximum(n, 1).astype(F32)
    large = max_exact + (jnp.log(nf / max_exact) / math.log(REL_MAX_DIST / max_exact) * (REL_BUCKETS - max_exact)).astype(jnp.int32)
    large = jnp.minimum(large, REL_BUCKETS - 1)
    return jnp.where(n < max_exact, n, large)


def moba_dense(q, k_all, v_all, qpos, rel_bias):
    B, T, H, Dh = k_all.shape
    L = q.shape[1]
    NB = -(-T // MOBA_BLOCK)
    pad = NB * MOBA_BLOCK - T
    kp = jnp.pad(k_all, ((0, 0), (0, pad), (0, 0), (0, 0)))
    vp = jnp.pad(v_all, ((0, 0), (0, pad), (0, 0), (0, 0)))
    block_mean = jnp.mean(kp.reshape(B, NB, MOBA_BLOCK, H, Dh), axis=2)
    own = qpos // MOBA_BLOCK
    gate = jnp.einsum('bqhd,bnhd->bhqn', q, block_mean, precision=lax.Precision.HIGHEST)
    eligible = (jnp.arange(NB)[None, :] < own[:, None])[None, None]
    gate = jnp.where(eligible, gate, -jnp.inf)
    gm = gate[..., None, :]
    gn = gate[..., :, None]
    idx = jnp.arange(NB)
    beats = (gm > gn) | ((gm == gn) & (idx[None, :] < idx[:, None]))
    rank = jnp.sum(beats & eligible[..., None, :], axis=-1)
    selected = eligible & (rank < MOBA_TOPK)
    kpos = jnp.arange(NB * MOBA_BLOCK)
    kblk = kpos // MOBA_BLOCK
    sel_key = jnp.take(selected, kblk, axis=-1)
    own_key = (kblk[None, :] == own[:, None]) & (kpos[None, :] <= qpos[:, None])
    valid = sel_key | own_key[None, None]
    dist = jnp.maximum(qpos[:, None] - kpos[None, :], 0)
    bias = rel_bias.astype(F32)[t5_bucket(dist)]
    logits = jnp.einsum('bqhd,bkhd->bhqk', q.astype(BF16), kp.astype(BF16), preferred_element_type=F32)
    logits = logits * (Dh ** -0.5) + bias.transpose(2, 0, 1)[None]
    logits = jnp.where(valid, logits, -jnp.inf)
    p = jax.nn.softmax(logits, axis=-1)
    o = jnp.einsum('bhqk,bkhd->bqhd', p.astype(BF16), vp.astype(BF16), preferred_element_type=F32)
    return o


def causal_dwconv(u, buf, w):
    ext = jnp.concatenate([buf.astype(u.dtype), u], axis=1)
    W = w.shape[0]
    L = u.shape[1]
    y = sum(ext[:, j:j + L] * w[j][None, None, :] for j in range(W))
    return y, ext[:, ext.shape[1] - (W - 1):]


def decoder_layer(x, c, pos, k_past, v_past, ret_state, sc_buf, cf_buf,
                  w_ada, b_ada, norm_g, w_in, w_gate, b_gate, ret_gn_w, sconv_w, cconv_w, cconv_b,
                  cnorm_w, cnorm_b, w_branch, w_o, w_ffn_in, w_ffn_out, rel_bias):
    B, L, D = x.shape
    BW = D // N_BRANCH
    M = B * L
    mod = (jax.nn.silu(c) @ w_ada + b_ada).reshape(B, N_MOD, D)[:, :, None, :]
    shift_m, scale_m, gate_m, shift_f, scale_f, gate_f = (mod[:, i] for i in range(N_MOD))
    h = rms_norm(x, norm_g[0]) * (1 + scale_m) + shift_m
    hb = h.astype(BF16).reshape(M, D)
    proj = matmul(hb, w_in).reshape(B, L, N_IN_SLOTS, BW)
    o_a, ret_new = retention_branch(proj[:, :, 0], proj[:, :, 1], proj[:, :, 2], proj[:, :, 3], pos, ret_state, ret_gn_w)
    H = BW // MOBA_HEAD_DIM
    q_b = proj[:, :, 4].reshape(B, L, H, MOBA_HEAD_DIM)
    k_b = proj[:, :, 5].reshape(B, L, H, MOBA_HEAD_DIM)
    v_b = proj[:, :, 6].reshape(B, L, H, MOBA_HEAD_DIM)
    k_all = jnp.concatenate([k_past, k_b], axis=1)
    v_all = jnp.concatenate([v_past, v_b], axis=1)
    o_b = moba_dense(q_b, k_all, v_all, pos, rel_bias).reshape(B, L, BW)
    y_c, sc_new = causal_dwconv(proj[:, :, 8] * proj[:, :, 9], sc_buf, sconv_w)
    o_c = proj[:, :, 7] * y_c
    u_d = proj[:, :, 10] * jax.nn.sigmoid(proj[:, :, 11])
    y_d, cf_new = causal_dwconv(u_d, cf_buf, cconv_w)
    o_d = jax.nn.silu(layer_norm(y_d + cconv_b, cnorm_w, cnorm_b))
    ups = [matmul(br.astype(BF16).reshape(M, BW), w_branch[n]) for n, br in enumerate((o_a, o_b, o_c, o_d))]
    gates = jax.nn.sigmoid(matmul(hb, w_gate) + b_gate).reshape(M, N_BRANCH, D)
    merged = sum(gates[:, n] * ups[n] for n in range(N_BRANCH))
    y = matmul(merged.astype(BF16), w_o).reshape(B, L, D)
    x = x + gate_m * rms_norm(y, norm_g[1])
    h = rms_norm(x, norm_g[2]) * (1 + scale_f) + shift_f
    ab = matmul(h.astype(BF16).reshape(M, D), w_ffn_in)
    F = ab.shape[1] // 2
    act = (jax.nn.silu(ab[:, :F]) * ab[:, F:]).astype(BF16)
    y2 = matmul(act, w_ffn_out, tn=512, tk=F // 2).reshape(B, L, D)
    x = x + gate_f * rms_norm(y2, norm_g[3])
    return x, (k_b, v_b, ret_new, sc_new, cf_new)


def kernel(x_prompt, x_sample, c_prompt, c_sample, cache_k, cache_v, state_ret, state_sconv, state_cconv, page_table, w_ada, b_ada, norm_g, w_in, w_gate, b_gate, ret_gn_w, sconv_w, cconv_w, cconv_b, cnorm_w, cnorm_b, w_branch, w_o, w_ffn_in, w_ffn_out, rel_bias):
    B, S, D = x_prompt.shape
    DB, DS, _ = x_sample.shape
    depth = w_in.shape[0]
    BW = D // N_BRANCH
    H = BW // MOBA_HEAD_DIM
    page = cache_k.shape[2]
    past_len = page_table.shape[1] * page
    pos_p = jnp.arange(S, dtype=jnp.int32)
    pos_s = past_len + jnp.arange(DS, dtype=jnp.int32)
    xp, xs = x_prompt, x_sample
    outs_p, outs_s = [], []
    for l in range(depth):
        lw = (w_ada[l], b_ada[l], norm_g[l], w_in[l].astype(BF16), w_gate[l].astype(BF16), b_gate[l], ret_gn_w[l],
              sconv_w[l], cconv_w[l], cconv_b[l], cnorm_w[l], cnorm_b[l], w_branch[l].astype(BF16),
              w_o[l].astype(BF16), w_ffn_in[l].astype(BF16), w_ffn_out[l].astype(BF16), rel_bias)
        empty_kv = jnp.zeros((B, 0, H, MOBA_HEAD_DIM), F32)
        xp, op = decoder_layer(
            xp, c_prompt, pos_p, empty_kv, empty_kv,
            jnp.zeros((B, RET_HEADS, BW // RET_HEADS, BW // RET_HEADS), F32),
            jnp.zeros((B, sconv_w.shape[1] - 1, BW), F32),
            jnp.zeros((B, cconv_w.shape[1] - 1, BW), F32), *lw)
        k_past = cache_k[l][page_table].reshape(DB, past_len, H, MOBA_HEAD_DIM)
        v_past = cache_v[l][page_table].reshape(DB, past_len, H, MOBA_HEAD_DIM)
        xs, os_ = decoder_layer(
            xs, c_sample, pos_s, k_past, v_past, state_ret[l], state_sconv[l], state_cconv[l], *lw)
        outs_p.append(op)
        outs_s.append(os_)
    stack = lambda outs, i: jnp.stack([o[i] for o in outs])
    new_k_prompt = stack(outs_p, 0).reshape(depth, B, S // page, page, H, MOBA_HEAD_DIM)
    new_v_prompt = stack(outs_p, 1).reshape(depth, B, S // page, page, H, MOBA_HEAD_DIM)
    return (xp, xs, new_k_prompt, new_v_prompt, stack(outs_s, 0), stack(outs_s, 1),
            stack(outs_p, 2), stack(outs_s, 2), stack(outs_p, 3), stack(outs_s, 3),
            stack(outs_p, 4), stack(outs_s, 4))
```

```python
import functools
import math

import jax
import jax.numpy as jnp
from jax import lax
from jax.experimental import pallas as pl
from jax.experimental.pallas import tpu as pltpu

F32 = jnp.float32
BF16 = jnp.bfloat16

N_BRANCH = 4
N_IN_SLOTS = 12
RET_HEADS = 4
RET_CHUNK = 128
ROPE_BASE = 10000.0
MOBA_HEAD_DIM = 128
MOBA_BLOCK = 256
MOBA_TOPK = 3
REL_BUCKETS = 32
REL_MAX_DIST = 128
N_MOD = 6
EPS = 1e-6
GN_EPS = 1e-5
CONV_PAD = 32

VMEM_LIMIT_BYTES = 56 * 1024 * 1024


def _params(*sem):
    return pltpu.CompilerParams(dimension_semantics=sem, vmem_limit_bytes=VMEM_LIMIT_BYTES)


def _pick(dim, prefs):
    for p in prefs:
        if dim % p == 0:
            return p
    return dim


def _bdot(a, b):
    return jnp.dot(a.astype(BF16), b.astype(BF16), preferred_element_type=F32)


def _bdot_nt(a, b):
    return lax.dot_general(a.astype(BF16), b.astype(BF16), (((1,), (1,)), ((), ())), preferred_element_type=F32)


def _bdot_tn(a, b):
    return lax.dot_general(a.astype(BF16), b.astype(BF16), (((0,), (0,)), ((), ())), preferred_element_type=F32)


def _mm_kernel(x_ref, w_ref, o_ref, *, nk):
    part = _bdot(x_ref[...], w_ref[...])
    if nk == 1:
        o_ref[...] = part.astype(o_ref.dtype)
    else:
        k = pl.program_id(2)

        @pl.when(k == 0)
        def _():
            o_ref[...] = part

        @pl.when(k > 0)
        def _():
            o_ref[...] += part


def matmul(x, w, *, tm=None, tn=None, tk=None):
    M, K = x.shape
    _, N = w.shape
    tm = tm or _pick(M, (1024, 512, 256, 128, 64))
    tn = tn or _pick(N, (1024, 512, 256, 128))
    tk = tk or K
    nk = K // tk
    return pl.pallas_call(
        functools.partial(_mm_kernel, nk=nk),
        grid=(M // tm, N // tn, nk),
        in_specs=[pl.BlockSpec((tm, tk), lambda i, j, k: (i, k)),
                  pl.BlockSpec((tk, tn), lambda i, j, k: (k, j))],
        out_specs=pl.BlockSpec((tm, tn), lambda i, j, k: (i, j)),
        out_shape=jax.ShapeDtypeStruct((M, N), F32),
        compiler_params=_params("parallel", "parallel", "arbitrary"),
        name="matmul",
    )(x, w)


def _ada_kernel(c_ref, w_ref, b_ref, o_ref):
    c = c_ref[...]
    x = c * jax.nn.sigmoid(c)
    o_ref[0] = _bdot(x, w_ref[0]) + b_ref[0]


def ada_mod(c_all, w_ada, b_ada):
    depth, D, N = w_ada.shape
    R = c_all.shape[0]
    tn = 512
    return pl.pallas_call(
        _ada_kernel,
        grid=(depth, N // tn),
        in_specs=[pl.BlockSpec((R, D), lambda l, j: (0, 0)),
                  pl.BlockSpec((1, D, tn), lambda l, j: (l, 0, j)),
                  pl.BlockSpec((1, 1, tn), lambda l, j: (l, 0, j))],
        out_specs=pl.BlockSpec((1, R, tn), lambda l, j: (l, 0, j)),
        out_shape=jax.ShapeDtypeStruct((depth, R, N), F32),
        compiler_params=_params("parallel", "parallel"),
        name="ada_mod",
    )(c_all, w_ada, b_ada.reshape(depth, 1, N))


def _rms(x, g):
    return x * lax.rsqrt(jnp.mean(x * x, axis=-1, keepdims=True) + EPS) * g


def _norm_mod_kernel(x_ref, g_ref, sc_ref, sh_ref, h_ref):
    h = _rms(x_ref[0], g_ref[...]) * (1.0 + sc_ref[0]) + sh_ref[0]
    h_ref[0] = h.astype(h_ref.dtype)


def _mod_spec(D, row0, slot):
    return pl.BlockSpec((1, 1, D), lambda b, t: ((row0 + b) * N_MOD + slot, 0, 0))


def norm_mod(x, g, mod, row0, slots):
    B, L, D = x.shape
    tl = _pick(L, (256, 128, 64, 32, 16, 8))
    return pl.pallas_call(
        _norm_mod_kernel,
        grid=(B, L // tl),
        in_specs=[pl.BlockSpec((1, tl, D), lambda b, t: (b, t, 0)),
                  pl.BlockSpec((1, D), lambda b, t: (0, 0)),
                  _mod_spec(D, row0, slots[0]), _mod_spec(D, row0, slots[1])],
        out_specs=pl.BlockSpec((1, tl, D), lambda b, t: (b, t, 0)),
        out_shape=jax.ShapeDtypeStruct((B, L, D), BF16),
        compiler_params=_params("parallel", "parallel"),
        name="norm_mod",
    )(x, g.reshape(1, D), mod, mod)


def _resid_kernel(x_ref, y_ref, ga_ref, gate_ref, *rest, emit_h):
    xn = x_ref[0] + gate_ref[0] * _rms(y_ref[0], ga_ref[...])
    if emit_h:
        gb_ref, sc_ref, sh_ref, xo_ref, h_ref = rest
        xo_ref[0] = xn
        h_ref[0] = (_rms(xn, gb_ref[...]) * (1.0 + sc_ref[0]) + sh_ref[0]).astype(h_ref.dtype)
    else:
        (xo_ref,) = rest
        xo_ref[0] = xn


def resid_norm(x, y, g_a, mod, row0, gate_slot, nxt=None):
    B, L, D = x.shape
    tl = _pick(L, (128, 64, 32, 16, 8))
    xspec = pl.BlockSpec((1, tl, D), lambda b, t: (b, t, 0))
    gspec = pl.BlockSpec((1, D), lambda b, t: (0, 0))
    in_specs = [xspec, xspec, gspec, _mod_spec(D, row0, gate_slot)]
    args = [x, y, g_a.reshape(1, D), mod]
    out_specs = [xspec]
    out_shape = [jax.ShapeDtypeStruct((B, L, D), F32)]
    if nxt is not None:
        g_b, mod_n, sc_slot, sh_slot = nxt
        in_specs += [gspec, _mod_spec(D, row0, sc_slot), _mod_spec(D, row0, sh_slot)]
        args += [g_b.reshape(1, D), mod_n, mod_n]
        out_specs.append(xspec)
        out_shape.append(jax.ShapeDtypeStruct((B, L, D), BF16))
    res = pl.pallas_call(
        functools.partial(_resid_kernel, emit_h=nxt is not None),
        grid=(B, L // tl),
        in_specs=in_specs, out_specs=out_specs, out_shape=out_shape,
        compiler_params=_params("parallel", "parallel"),
        name="resid_norm",
    )(*args)
    return (res[0], res[1]) if nxt is not None else (res[0], None)


def _ret_kernel(q_ref, k_ref, v_ref, g_ref, cos_ref, sin_ref, dec_ref, qd_ref, kd_ref, gl_ref, s0_ref, gnw_ref,
                o_ref, so_ref, s_scr, *, nc, dh):
    c = pl.program_id(2)

    @pl.when(c == 0)
    def _():
        s_scr[...] = s0_ref[0, 0]

    half = dh // 2
    cos = cos_ref[...]
    sin = sin_ref[...]

    def rot(x):
        x1 = x[:, :half]
        x2 = x[:, half:]
        return jnp.concatenate([x1 * cos - x2 * sin, x1 * sin + x2 * cos], axis=-1)

    q = rot(q_ref[0])
    k = rot(k_ref[0]) * (dh ** -0.5)
    v = v_ref[0]
    S = s_scr[...]
    scores = _bdot_nt(q, k) * dec_ref[0]
    o = _bdot(scores, v) + _bdot(q * qd_ref[0], S)
    S_new = gl_ref[0, 0:1, :] * S + _bdot_tn(k * kd_ref[0], v)
    s_scr[...] = S_new

    @pl.when(c == nc - 1)
    def _():
        so_ref[0, 0] = S_new

    oc = o - jnp.mean(o, axis=-1, keepdims=True)
    on = oc * lax.rsqrt(jnp.mean(oc * oc, axis=-1, keepdims=True) + GN_EPS)
    g = g_ref[0]
    o_ref[0] = (on * gnw_ref[...] * (g * jax.nn.sigmoid(g))).astype(o_ref.dtype)


def retention(proj, pos, state0, gn_w):
    B, L, _ = proj.shape
    H = RET_HEADS
    BW = gn_w.shape[0]
    dh = BW // H
    half = dh // 2
    C = RET_CHUNK if L % RET_CHUNK == 0 else L
    nc = L // C
    inv = jnp.exp(-math.log(ROPE_BASE) * jnp.arange(half, dtype=F32) / half)
    ang = pos.astype(F32)[:, None] * inv[None, :]
    cos, sin = jnp.cos(ang), jnp.sin(ang)
    log_g = jnp.log1p(-jnp.exp2(-5.0 - jnp.arange(H, dtype=F32)))
    idx = jnp.arange(C, dtype=F32)
    diff = idx[:, None] - idx[None, :]
    causal = diff >= 0
    dec = jnp.where(causal[None], jnp.exp(log_g[:, None, None] * jnp.where(causal, diff, 0.0)[None]), 0.0)
    qd = jnp.broadcast_to(jnp.exp(log_g[:, None] * (idx[None, :] + 1.0))[:, :, None], (H, C, dh))
    kd = jnp.broadcast_to(jnp.exp(log_g[:, None] * ((C - 1.0) - idx[None, :]))[:, :, None], (H, C, dh))
    gl = jnp.broadcast_to(jnp.exp(log_g * C)[:, None, None], (H, 8, dh))
    hb = BW // dh

    def slot(s):
        return pl.BlockSpec((1, C, dh), lambda b, h, c: (b, c, s * hb + h))

    tab = pl.BlockSpec((C, half), lambda b, h, c: (c, 0))
    return pl.pallas_call(
        functools.partial(_ret_kernel, nc=nc, dh=dh),
        grid=(B, H, nc),
        in_specs=[slot(0), slot(1), slot(2), slot(3), tab, tab,
                  pl.BlockSpec((1, C, C), lambda b, h, c: (h, 0, 0)),
                  pl.BlockSpec((1, C, dh), lambda b, h, c: (h, 0, 0)),
                  pl.BlockSpec((1, C, dh), lambda b, h, c: (h, 0, 0)),
                  pl.BlockSpec((1, 8, dh), lambda b, h, c: (h, 0, 0)),
                  pl.BlockSpec((1, 1, dh, dh), lambda b, h, c: (b, h, 0, 0)),
                  pl.BlockSpec((1, dh), lambda b, h, c: (0, h))],
        out_specs=[pl.BlockSpec((1, C, dh), lambda b, h, c: (b, c, h)),
                   pl.BlockSpec((1, 1, dh, dh), lambda b, h, c: (b, h, 0, 0))],
        out_shape=[jax.ShapeDtypeStruct((B, L, BW), BF16),
                   jax.ShapeDtypeStruct((B, H, dh, dh), F32)],
        scratch_shapes=[pltpu.VMEM((dh, dh), F32)],
        compiler_params=_params("parallel", "parallel", "arbitrary"),
        name="retention",
    )(proj, proj, proj, proj, cos, sin, dec, qd, kd, gl, state0, gn_w.reshape(1, BW))


def t5_bucket(n):
    max_exact = REL_BUCKETS // 2
    nf = jnp.maximum(n, 1).astype(F32)
    large = max_exact + (jnp.log(nf / max_exact) / math.log(REL_MAX_DIST / max_exact) * (REL_BUCKETS - max_exact)).astype(jnp.int32)
    large = jnp.minimum(large, REL_BUCKETS - 1)
    return jnp.where(n < max_exact, n, large)


def _select_blocks(gate, n_elig, nb):
    col = lax.broadcasted_iota(jnp.int32, gate.shape, 1)
    rank = jnp.zeros(gate.shape, jnp.int32)
    for m in range(nb):
        gm = gate[:, m:m + 1]
        beats = (gm > gate) | ((gm == gate) & (m < col))
        rank = rank + jnp.where(beats, jnp.where(m < n_elig, 1, 0), 0)
    return (col < n_elig) & (rank < MOBA_TOPK)


def _moba_prompt_kernel(q_ref, k_ref, v_ref, tb_ref, o_ref, kbar_scr, *, nb, blk, scale):
    i = pl.program_id(2)

    @pl.when(i == 0)
    def _():
        for j in range(nb):
            kbar_scr[j:j + 1, :] = jnp.mean(k_ref[0, j * blk:(j + 1) * blk, :], axis=0, keepdims=True)

    q = q_ref[0]
    gate = lax.dot_general(q, kbar_scr[...], (((1,), (1,)), ((), ())),
                           precision=lax.Precision.HIGHEST, preferred_element_type=F32)
    sel = _select_blocks(gate, i, nb)
    col = lax.broadcasted_iota(jnp.int32, sel.shape, 1)
    qb = q.astype(BF16)

    def sel_col(j):
        return jnp.sum(jnp.where((col == j) & sel, 1.0, 0.0), axis=1, keepdims=True) > 0.0

    def tile(j):
        start = pl.multiple_of(j * blk, blk)
        kj = k_ref[0, pl.ds(start, blk), :]
        vj = v_ref[0, pl.ds(start, blk), :]
        return _bdot_nt(qb, kj) * scale, vj

    s, vj = tile(i)
    r = lax.broadcasted_iota(jnp.int32, (blk, blk), 0)
    cidx = lax.broadcasted_iota(jnp.int32, (blk, blk), 1)
    s = jnp.where(cidx <= r, s + tb_ref[0, :, blk:], -jnp.inf)
    m = jnp.max(s, axis=-1, keepdims=True)
    p = jnp.exp(s - m)
    l = jnp.sum(p, axis=-1, keepdims=True)
    acc = _bdot(p, vj)

    def update(carry, s, vj):
        m, l, acc = carry
        m_new = jnp.maximum(m, jnp.max(s, axis=-1, keepdims=True))
        a = jnp.exp(m - m_new)
        p = jnp.exp(s - m_new)
        return m_new, a * l + jnp.sum(p, axis=-1, keepdims=True), a * acc + _bdot(p, vj)

    jp = jnp.maximum(i - 1, 0)
    s, vj = tile(jp)
    s = jnp.where(sel_col(jp), s + tb_ref[0, :, :blk], -jnp.inf)
    carry = update((m, l, acc), s, vj)

    far_bias = tb_ref[0, 0:1, 0:1]

    def body(j, carry):
        s, vj = tile(j)
        s = jnp.where(sel_col(j), s + far_bias, -jnp.inf)
        return update(carry, s, vj)

    m, l, acc = lax.fori_loop(0, jp, body, carry)
    o_ref[0] = (acc / l).astype(o_ref.dtype)


def moba_prompt(proj, rel_bias, H):
    B, L, _ = proj.shape
    blk = MOBA_BLOCK
    dh = MOBA_HEAD_DIM
    nb = L // blk
    assert L % blk == 0 and 2 * blk > REL_MAX_DIST
    r = jnp.arange(blk)[:, None]
    c = jnp.arange(2 * blk)[None, :]
    dist = jnp.maximum(blk + r - c, 0)
    tb = rel_bias.astype(F32)[t5_bucket(dist)].transpose(2, 0, 1)
    return pl.pallas_call(
        functools.partial(_moba_prompt_kernel, nb=nb, blk=blk, scale=dh ** -0.5),
        grid=(B, H, nb),
        in_specs=[pl.BlockSpec((1, blk, dh), lambda b, h, i: (b, i, 4 * H + h)),
                  pl.BlockSpec((1, L, dh), lambda b, h, i: (b, 0, 5 * H + h)),
                  pl.BlockSpec((1, L, dh), lambda b, h, i: (b, 0, 6 * H + h)),
                  pl.BlockSpec((1, blk, 2 * blk), lambda b, h, i: (h, 0, 0))],
        out_specs=pl.BlockSpec((1, blk, dh), lambda b, h, i: (b, i, h)),
        out_shape=jax.ShapeDtypeStruct((B, L, H * dh), BF16),
        scratch_shapes=[pltpu.VMEM((nb, dh), F32)],
        compiler_params=_params("parallel", "parallel", "arbitrary"),
        name="moba_prompt",
    )(proj, proj, proj, tb)


def moba_dense(q, k_all, v_all, qpos, rel_bias):
    B, T, H, Dh = k_all.shape
    NB = -(-T // MOBA_BLOCK)
    pad = NB * MOBA_BLOCK - T
    kp = jnp.pad(k_all, ((0, 0), (0, pad), (0, 0), (0, 0)))
    vp = jnp.pad(v_all, ((0, 0), (0, pad), (0, 0), (0, 0)))
    block_mean = jnp.mean(kp.reshape(B, NB, MOBA_BLOCK, H, Dh), axis=2)
    own = qpos // MOBA_BLOCK
    gate = jnp.einsum('bqhd,bnhd->bhqn', q, block_mean, precision=lax.Precision.HIGHEST)
    eligible = (jnp.arange(NB)[None, :] < own[:, None])[None, None]
    gate = jnp.where(eligible, gate, -jnp.inf)
    gm = gate[..., None, :]
    gn = gate[..., :, None]
    idx = jnp.arange(NB)
    beats = (gm > gn) | ((gm == gn) & (idx[None, :] < idx[:, None]))
    rank = jnp.sum(beats & eligible[..., None, :], axis=-1)
    selected = eligible & (rank < MOBA_TOPK)
    kpos = jnp.arange(NB * MOBA_BLOCK)
    kblk = kpos // MOBA_BLOCK
    sel_key = jnp.repeat(selected, MOBA_BLOCK, axis=-1)
    own_key = (kblk[None, :] == own[:, None]) & (kpos[None, :] <= qpos[:, None])
    valid = sel_key | own_key[None, None]
    dist = jnp.maximum(qpos[:, None] - kpos[None, :], 0)
    bias = rel_bias.astype(F32)[t5_bucket(dist)]
    logits = jnp.einsum('bqhd,bkhd->bhqk', q.astype(BF16), kp.astype(BF16), preferred_element_type=F32)
    logits = logits * (Dh ** -0.5) + bias.transpose(2, 0, 1)[None]
    logits = jnp.where(valid, logits, -jnp.inf)
    p = jax.nn.softmax(logits, axis=-1)
    return jnp.einsum('bhqk,bkhd->bqhd', p.astype(BF16), vp.astype(BF16), preferred_element_type=F32)


def _conv_kernel(p7, p8, p9, p10, p11, scb, cfb, sw, cw, cb, nw, nbias, oc_ref, od_ref, scn_ref, cfn_ref,
                 extc, extd, ybuf, *, tl, nt, ws, wd):
    t = pl.program_id(1)
    P = CONV_PAD
    W = ybuf.shape[1]

    @pl.when(t == 0)
    def _():
        extc[pl.ds(P - (ws - 1), ws - 1), :] = scb[0]
        extd[pl.ds(P - (wd - 1), wd - 1), :] = cfb[0]

    extc[pl.ds(P, tl), :] = p8[0] * p9[0]
    extd[pl.ds(P, tl), :] = p10[0] * jax.nn.sigmoid(p11[0])

    yc = sw[0:1, :] * extc[pl.ds(P - (ws - 1), tl), :]
    for j in range(1, ws):
        yc = yc + sw[j:j + 1, :] * extc[pl.ds(P - (ws - 1) + j, tl), :]
    oc_ref[0] = (p7[0] * yc).astype(oc_ref.dtype)

    for c0 in range(0, W, 128):
        cols = slice(c0, c0 + 128)
        acc = cw[0:1, cols] * extd[pl.ds(P - (wd - 1), tl), cols]
        for j in range(1, wd):
            acc = acc + cw[j:j + 1, cols] * extd[pl.ds(P - (wd - 1) + j, tl), cols]
        ybuf[:, cols] = acc + cb[:, cols]
    y = ybuf[...]
    yc_ = y - jnp.mean(y, axis=-1, keepdims=True)
    yn = yc_ * lax.rsqrt(jnp.mean(yc_ * yc_, axis=-1, keepdims=True) + EPS) * nw[...] + nbias[...]
    od_ref[0] = (yn * jax.nn.sigmoid(yn)).astype(od_ref.dtype)

    newc = extc[pl.ds(P + tl - (ws - 1), ws - 1), :]
    newd = extd[pl.ds(P + tl - (wd - 1), wd - 1), :]
    extc[pl.ds(P - (ws - 1), ws - 1), :] = newc
    extd[pl.ds(P - (wd - 1), wd - 1), :] = newd

    @pl.when(t == nt - 1)
    def _():
        scn_ref[0] = newc
        cfn_ref[0] = newd


def conv_branches(proj, sc_buf, cf_buf, sconv_w, cconv_w, cconv_b, cnorm_w, cnorm_b):
    B, L, _ = proj.shape
    ws, BW = sconv_w.shape
    wd = cconv_w.shape[0]
    assert wd - 1 <= CONV_PAD
    tl = _pick(L, (128, 64, 32, 16, 8))
    nt = L // tl

    def slot(s):
        return pl.BlockSpec((1, tl, BW), lambda b, t: (b, t, s))

    full = lambda a: pl.BlockSpec(a.shape, lambda b, t: (0,) * a.ndim)
    cb2, nw2, nb2 = cconv_b.reshape(1, BW), cnorm_w.reshape(1, BW), cnorm_b.reshape(1, BW)
    ospec = pl.BlockSpec((1, tl, BW), lambda b, t: (b, t, 0))
    return pl.pallas_call(
        functools.partial(_conv_kernel, tl=tl, nt=nt, ws=ws, wd=wd),
        grid=(B, nt),
        in_specs=[slot(7), slot(8), slot(9), slot(10), slot(11),
                  pl.BlockSpec((1, ws - 1, BW), lambda b, t: (b, 0, 0)),
                  pl.BlockSpec((1, wd - 1, BW), lambda b, t: (b, 0, 0)),
                  full(sconv_w), full(cconv_w), full(cb2), full(nw2), full(nb2)],
        out_specs=[ospec, ospec,
                   pl.BlockSpec((1, ws - 1, BW), lambda b, t: (b, 0, 0)),
                   pl.BlockSpec((1, wd - 1, BW), lambda b, t: (b, 0, 0))],
        out_shape=[jax.ShapeDtypeStruct((B, L, BW), BF16), jax.ShapeDtypeStruct((B, L, BW), BF16),
                   jax.ShapeDtypeStruct((B, ws - 1, BW), F32), jax.ShapeDtypeStruct((B, wd - 1, BW), F32)],
        scratch_shapes=[pltpu.VMEM((CONV_PAD + tl, BW), F32), pltpu.VMEM((CONV_PAD + tl, BW), F32),
                        pltpu.VMEM((tl, BW), F32)],
        compiler_params=_params("parallel", "arbitrary"),
        name="conv_branches",
    )(proj, proj, proj, proj, proj, sc_buf, cf_buf, sconv_w, cconv_w, cb2, nw2, nb2)


def _gate_merge_kernel(h_ref, *refs):
    br = refs[0:4]
    wg = refs[4:8]
    wb = refs[8:12]
    bg = refs[12:16]
    o_ref = refs[16]
    h = h_ref[...]
    acc = None
    for n in range(N_BRANCH):
        gate = jax.nn.sigmoid(_bdot(h, wg[n][...]) + bg[n][0])
        term = gate * _bdot(br[n][...], wb[n][0])
        acc = term if acc is None else acc + term
    o_ref[...] = acc.astype(o_ref.dtype)


def gate_merge(h, branches, w_gate, b_gate, w_branch):
    M, D = h.shape
    BW = branches[0].shape[1]
    tm = _pick(M, (512, 256, 128, 64))
    tn = 256
    nj = D // tn
    in_specs = [pl.BlockSpec((tm, D), lambda i, j: (i, 0))]
    in_specs += [pl.BlockSpec((tm, BW), lambda i, j: (i, 0)) for _ in range(N_BRANCH)]
    in_specs += [pl.BlockSpec((D, tn), functools.partial(lambda i, j, n: (0, n * nj + j), n=n)) for n in range(N_BRANCH)]
    in_specs += [pl.BlockSpec((1, BW, tn), functools.partial(lambda i, j, n: (n, 0, j), n=n)) for n in range(N_BRANCH)]
    in_specs += [pl.BlockSpec((1, 1, tn), functools.partial(lambda i, j, n: (n, 0, j), n=n)) for n in range(N_BRANCH)]
    bg3 = b_gate.reshape(N_BRANCH, 1, D)
    return pl.pallas_call(
        _gate_merge_kernel,
        grid=(M // tm, nj),
        in_specs=in_specs,
        out_specs=pl.BlockSpec((tm, tn), lambda i, j: (i, j)),
        out_shape=jax.ShapeDtypeStruct((M, D), BF16),
        compiler_params=_params("parallel", "parallel"),
        name="gate_merge",
    )(h, *branches, *([w_gate] * N_BRANCH), *([w_branch] * N_BRANCH), *([bg3] * N_BRANCH))


def _ffn_in_kernel(h_ref, wa_ref, wb_ref, o_ref):
    h = h_ref[...]
    a = _bdot(h, wa_ref[...])
    b = _bdot(h, wb_ref[...])
    o_ref[...] = (a * jax.nn.sigmoid(a) * b).astype(o_ref.dtype)


def ffn_in(h, w):
    M, D = h.shape
    F = w.shape[1] // 2
    tm = _pick(M, (1024, 512, 256, 128, 64))
    tn = 256
    nj = F // tn
    return pl.pallas_call(
        _ffn_in_kernel,
        grid=(M // tm, nj),
        in_specs=[pl.BlockSpec((tm, D), lambda i, j: (i, 0)),
                  pl.BlockSpec((D, tn), lambda i, j: (0, j)),
                  pl.BlockSpec((D, tn), lambda i, j: (0, nj + j))],
        out_specs=pl.BlockSpec((tm, tn), lambda i, j: (i, j)),
        out_shape=jax.ShapeDtypeStruct((M, F), BF16),
        compiler_params=_params("parallel", "parallel"),
        name="ffn_in",
    )(h, w, w)


def decoder_layer(x, h, pos, past, ret_state, sc_buf, cf_buf, mod, row0, nxt, lw):
    (norm_g, w_in, w_gate, b_gate, ret_gn_w, sconv_w, cconv_w, cconv_b, cnorm_w, cnorm_b,
     w_branch, w_o, w_ffn_in, w_ffn_out, rel_bias) = lw
    B, L, D = x.shape
    BW = D // N_BRANCH
    H = BW // MOBA_HEAD_DIM
    M = B * L
    F = w_ffn_out.shape[0]
    h2d = h.reshape(M, D)
    proj = matmul(h2d, w_in).reshape(B, L, N_IN_SLOTS * BW)
    o_a, ret_new = retention(proj, pos, ret_state, ret_gn_w)
    k_b = proj[:, :, 5 * BW:6 * BW].reshape(B, L, H, MOBA_HEAD_DIM)
    v_b = proj[:, :, 6 * BW:7 * BW].reshape(B, L, H, MOBA_HEAD_DIM)
    if past is None:
        o_b = moba_prompt(proj, rel_bias, H)
    else:
        k_past, v_past = past
        q_b = proj[:, :, 4 * BW:5 * BW].reshape(B, L, H, MOBA_HEAD_DIM)
        k_all = jnp.concatenate([k_past, k_b], axis=1)
        v_all = jnp.concatenate([v_past, v_b], axis=1)
        o_b = moba_dense(q_b, k_all, v_all, pos, rel_bias).reshape(B, L, BW).astype(BF16)
    o_c, o_d, sc_new, cf_new = conv_branches(proj, sc_buf, cf_buf, sconv_w, cconv_w, cconv_b, cnorm_w, cnorm_b)
    branches = [a.reshape(M, BW) for a in (o_a, o_b, o_c, o_d)]
    merged = gate_merge(h2d, branches, w_gate, b_gate, w_branch)
    y = matmul(merged, w_o).reshape(B, L, D)
    x, h = resid_norm(x, y, norm_g[1], mod, row0, 2, nxt=(norm_g[2], mod, 4, 3))
    act = ffn_in(h.reshape(M, D), w_ffn_in)
    y2 = matmul(act, w_ffn_out, tn=512, tk=F // 2).reshape(B, L, D)
    x, h = resid_norm(x, y2, norm_g[3], mod, row0, 5, nxt=nxt)
    return x, h, (k_b, v_b, ret_new, sc_new, cf_new)


def kernel(x_prompt, x_sample, c_prompt, c_sample, cache_k, cache_v, state_ret, state_sconv, state_cconv, page_table, w_ada, b_ada, norm_g, w_in, w_gate, b_gate, ret_gn_w, sconv_w, cconv_w, cconv_b, cnorm_w, cnorm_b, w_branch, w_o, w_ffn_in, w_ffn_out, rel_bias):
    B, S, D = x_prompt.shape
    DB, DS, _ = x_sample.shape
    depth = w_in.shape[0]
    BW = D // N_BRANCH
    H = BW // MOBA_HEAD_DIM
    dh_ret = BW // RET_HEADS
    page = cache_k.shape[2]
    past_len = page_table.shape[1] * page
    pos_p = jnp.arange(S, dtype=jnp.int32)
    pos_s = past_len + jnp.arange(DS, dtype=jnp.int32)

    rows = B + DB
    rows_pad = -(-rows // 8) * 8
    c_all = jnp.concatenate([c_prompt, c_sample, jnp.zeros((rows_pad - rows, D), F32)], axis=0)
    mods = ada_mod(c_all, w_ada, b_ada).reshape(depth, rows_pad * N_MOD, 1, D)

    xp, xs = x_prompt, x_sample
    hp = norm_mod(xp, norm_g[0, 0], mods[0], 0, (1, 0))
    hs = norm_mod(xs, norm_g[0, 0], mods[0], B, (1, 0))
    zeros_ret = jnp.zeros((B, RET_HEADS, dh_ret, dh_ret), F32)
    zeros_sc = jnp.zeros((B, sconv_w.shape[1] - 1, BW), F32)
    zeros_cf = jnp.zeros((B, cconv_w.shape[1] - 1, BW), F32)
    outs_p, outs_s = [], []
    for l in range(depth):
        lw = (norm_g[l], w_in[l].astype(BF16), w_gate[l].astype(BF16), b_gate[l], ret_gn_w[l],
              sconv_w[l], cconv_w[l], cconv_b[l], cnorm_w[l], cnorm_b[l], w_branch[l].astype(BF16),
              w_o[l].astype(BF16), w_ffn_in[l].astype(BF16), w_ffn_out[l].astype(BF16), rel_bias)
        nxt = (norm_g[l + 1, 0], mods[l + 1], 1, 0) if l + 1 < depth else None
        xp, hp, op = decoder_layer(xp, hp, pos_p, None, zeros_ret, zeros_sc, zeros_cf, mods[l], 0, nxt, lw)
        k_past = cache_k[l][page_table].reshape(DB, past_len, H, MOBA_HEAD_DIM)
        v_past = cache_v[l][page_table].reshape(DB, past_len, H, MOBA_HEAD_DIM)
        xs, hs, os_ = decoder_layer(xs, hs, pos_s, (k_past, v_past), state_ret[l], state_sconv[l], state_cconv[l],
                                    mods[l], B, nxt, lw)
        outs_p.append(op)
        outs_s.append(os_)
    stack = lambda outs, i: jnp.stack([o[i] for o in outs])
    new_k_prompt = stack(outs_p, 0).reshape(depth, B, S // page, page, H, MOBA_HEAD_DIM)
    new_v_prompt = stack(outs_p, 1).reshape(depth, B, S // page, page, H, MOBA_HEAD_DIM)
    return (xp, xs, new_k_prompt, new_v_prompt, stack(outs_s, 0), stack(outs_s, 1),
            stack(outs_p, 2), stack(outs_s, 2), stack(outs_p, 3), stack(outs_s, 3),
            stack(outs_p, 4), stack(outs_s, 4))
```

```python
import functools
import math

import jax
import jax.numpy as jnp
from jax import lax
from jax.experimental import pallas as pl
from jax.experimental.pallas import tpu as pltpu

F32 = jnp.float32
BF16 = jnp.bfloat16

N_BRANCH = 4
N_IN_SLOTS = 12
RET_HEADS = 4
RET_CHUNK = 128
ROPE_BASE = 10000.0
MOBA_HEAD_DIM = 128
MOBA_BLOCK = 256
MOBA_TOPK = 3
REL_BUCKETS = 32
REL_MAX_DIST = 128
N_MOD = 6
EPS = 1e-6
GN_EPS = 1e-5
CONV_PAD = 32
LANES = 128
KV_PAGES_PER_STEP = 8

VMEM_LIMIT_BYTES = 56 * 1024 * 1024


def _params(*sem):
    return pltpu.CompilerParams(dimension_semantics=sem, vmem_limit_bytes=VMEM_LIMIT_BYTES)


def _pick(dim, prefs):
    for p in prefs:
        if dim % p == 0:
            return p
    return dim


def _bdot(a, b):
    return jnp.dot(a.astype(BF16), b.astype(BF16), preferred_element_type=F32)


def _bdot_nt(a, b):
    return lax.dot_general(a.astype(BF16), b.astype(BF16), (((1,), (1,)), ((), ())), preferred_element_type=F32)


def _bdot_tn(a, b):
    return lax.dot_general(a.astype(BF16), b.astype(BF16), (((0,), (0,)), ((), ())), preferred_element_type=F32)


def _fdot_nt(a, b):
    return lax.dot_general(a, b, (((1,), (1,)), ((), ())), precision=lax.Precision.HIGHEST, preferred_element_type=F32)


def _mm_kernel(x_ref, w_ref, o_ref, *, nk):
    part = _bdot(x_ref[...], w_ref[...])
    if nk == 1:
        o_ref[...] = part.astype(o_ref.dtype)
    else:
        k = pl.program_id(2)

        @pl.when(k == 0)
        def _():
            o_ref[...] = part

        @pl.when(k > 0)
        def _():
            o_ref[...] += part


def matmul(x, w, layer, *, tm=None, tn=None, tk=None):
    M, K = x.shape
    N = w.shape[2]
    tm = tm or _pick(M, (1024, 512, 256, 128, 64))
    tn = tn or _pick(N, (1024, 512, 256, 128))
    tk = tk or K
    nk = K // tk
    return pl.pallas_call(
        functools.partial(_mm_kernel, nk=nk),
        grid=(M // tm, N // tn, nk),
        in_specs=[pl.BlockSpec((tm, tk), lambda i, j, k: (i, k)),
                  pl.BlockSpec((None, tk, tn), lambda i, j, k: (layer, k, j))],
        out_specs=pl.BlockSpec((tm, tn), lambda i, j, k: (i, j)),
        out_shape=jax.ShapeDtypeStruct((M, N), F32),
        compiler_params=_params("parallel", "parallel", "arbitrary"),
        name="matmul",
    )(x, w)


def _ada_kernel(c_ref, w_ref, b_ref, o_ref):
    c = c_ref[...]
    x = c * jax.nn.sigmoid(c)
    o_ref[0] = _bdot(x, w_ref[0]) + b_ref[0]


def ada_mod(c_all, w_ada, b_ada):
    depth, D, N = w_ada.shape
    R = c_all.shape[0]
    tn = 512
    return pl.pallas_call(
        _ada_kernel,
        grid=(depth, N // tn),
        in_specs=[pl.BlockSpec((R, D), lambda l, j: (0, 0)),
                  pl.BlockSpec((1, D, tn), lambda l, j: (l, 0, j)),
                  pl.BlockSpec((1, 1, tn), lambda l, j: (l, 0, j))],
        out_specs=pl.BlockSpec((1, R, tn), lambda l, j: (l, 0, j)),
        out_shape=jax.ShapeDtypeStruct((depth, R, N), F32),
        compiler_params=_params("parallel", "parallel"),
        name="ada_mod",
    )(c_all, w_ada, b_ada.reshape(depth, 1, N))


def _rms(x, g):
    return x * lax.rsqrt(jnp.mean(x * x, axis=-1, keepdims=True) + EPS) * g


def _norm_mod_kernel(x_ref, g_ref, sc_ref, sh_ref, h_ref):
    h = _rms(x_ref[0], g_ref[...]) * (1.0 + sc_ref[0]) + sh_ref[0]
    h_ref[0] = h.astype(h_ref.dtype)


def _mod_spec(D, row0, slot):
    return pl.BlockSpec((1, 1, D), lambda b, t: ((row0 + b) * N_MOD + slot, 0, 0))


def norm_mod(x, g, mod, row0, slots):
    B, L, D = x.shape
    tl = _pick(L, (256, 128, 64, 32, 16, 8))
    return pl.pallas_call(
        _norm_mod_kernel,
        grid=(B, L // tl),
        in_specs=[pl.BlockSpec((1, tl, D), lambda b, t: (b, t, 0)),
                  pl.BlockSpec((1, D), lambda b, t: (0, 0)),
                  _mod_spec(D, row0, slots[0]), _mod_spec(D, row0, slots[1])],
        out_specs=pl.BlockSpec((1, tl, D), lambda b, t: (b, t, 0)),
        out_shape=jax.ShapeDtypeStruct((B, L, D), BF16),
        compiler_params=_params("parallel", "parallel"),
        name="norm_mod",
    )(x, g.reshape(1, D), mod, mod)


def _resid_kernel(x_ref, y_ref, ga_ref, gate_ref, *rest, emit_h):
    xn = x_ref[0] + gate_ref[0] * _rms(y_ref[0], ga_ref[...])
    if emit_h:
        gb_ref, sc_ref, sh_ref, xo_ref, h_ref = rest
        xo_ref[0] = xn
        h_ref[0] = (_rms(xn, gb_ref[...]) * (1.0 + sc_ref[0]) + sh_ref[0]).astype(h_ref.dtype)
    else:
        (xo_ref,) = rest
        xo_ref[0] = xn


def resid_norm(x, y, g_a, mod, row0, gate_slot, nxt=None):
    B, L, D = x.shape
    tl = _pick(L, (128, 64, 32, 16, 8))
    xspec = pl.BlockSpec((1, tl, D), lambda b, t: (b, t, 0))
    gspec = pl.BlockSpec((1, D), lambda b, t: (0, 0))
    in_specs = [xspec, xspec, gspec, _mod_spec(D, row0, gate_slot)]
    args = [x, y, g_a.reshape(1, D), mod]
    out_specs = [xspec]
    out_shape = [jax.ShapeDtypeStruct((B, L, D), F32)]
    if nxt is not None:
        g_b, mod_n, sc_slot, sh_slot = nxt
        in_specs += [gspec, _mod_spec(D, row0, sc_slot), _mod_spec(D, row0, sh_slot)]
        args += [g_b.reshape(1, D), mod_n, mod_n]
        out_specs.append(xspec)
        out_shape.append(jax.ShapeDtypeStruct((B, L, D), BF16))
    res = pl.pallas_call(
        functools.partial(_resid_kernel, emit_h=nxt is not None),
        grid=(B, L // tl),
        in_specs=in_specs, out_specs=out_specs, out_shape=out_shape,
        compiler_params=_params("parallel", "parallel"),
        name="resid_norm",
    )(*args)
    return (res[0], res[1]) if nxt is not None else (res[0], None)


def _ret_kernel(q_ref, k_ref, v_ref, g_ref, cos_ref, sin_ref, dec_ref, qd_ref, kd_ref, gl_ref, s0_ref, gnw_ref,
                o_ref, so_ref, s_scr, *, nc, dh):
    c = pl.program_id(2)

    @pl.when(c == 0)
    def _():
        s_scr[...] = s0_ref[0, 0]

    half = dh // 2
    cos = cos_ref[...]
    sin = sin_ref[...]

    def rot(x):
        x1 = x[:, :half]
        x2 = x[:, half:]
        return jnp.concatenate([x1 * cos - x2 * sin, x1 * sin + x2 * cos], axis=-1)

    q = rot(q_ref[0])
    k = rot(k_ref[0]) * (dh ** -0.5)
    v = v_ref[0]
    S = s_scr[...]
    scores = _bdot_nt(q, k) * dec_ref[0]
    o = _bdot(scores, v) + _bdot(q * qd_ref[0], S)
    S_new = gl_ref[0, 0:1, :] * S + _bdot_tn(k * kd_ref[0], v)
    s_scr[...] = S_new

    @pl.when(c == nc - 1)
    def _():
        so_ref[0, 0] = S_new

    oc = o - jnp.mean(o, axis=-1, keepdims=True)
    on = oc * lax.rsqrt(jnp.mean(oc * oc, axis=-1, keepdims=True) + GN_EPS)
    g = g_ref[0]
    o_ref[0] = (on * gnw_ref[...] * (g * jax.nn.sigmoid(g))).astype(o_ref.dtype)


def retention(proj, pos, state0, gn_w):
    B, L, _ = proj.shape
    H = RET_HEADS
    BW = gn_w.shape[0]
    dh = BW // H
    half = dh // 2
    C = RET_CHUNK if L % RET_CHUNK == 0 else L
    nc = L // C
    inv = jnp.exp(-math.log(ROPE_BASE) * jnp.arange(half, dtype=F32) / half)
    ang = pos.astype(F32)[:, None] * inv[None, :]
    cos, sin = jnp.cos(ang), jnp.sin(ang)
    log_g = jnp.log1p(-jnp.exp2(-5.0 - jnp.arange(H, dtype=F32)))
    idx = jnp.arange(C, dtype=F32)
    diff = idx[:, None] - idx[None, :]
    causal = diff >= 0
    dec = jnp.where(causal[None], jnp.exp(log_g[:, None, None] * jnp.where(causal, diff, 0.0)[None]), 0.0)
    qd = jnp.broadcast_to(jnp.exp(log_g[:, None] * (idx[None, :] + 1.0))[:, :, None], (H, C, dh))
    kd = jnp.broadcast_to(jnp.exp(log_g[:, None] * ((C - 1.0) - idx[None, :]))[:, :, None], (H, C, dh))
    gl = jnp.broadcast_to(jnp.exp(log_g * C)[:, None, None], (H, 8, dh))
    hb = BW // dh

    def slot(s):
        return pl.BlockSpec((1, C, dh), lambda b, h, c: (b, c, s * hb + h))

    tab = pl.BlockSpec((C, half), lambda b, h, c: (c, 0))
    return pl.pallas_call(
        functools.partial(_ret_kernel, nc=nc, dh=dh),
        grid=(B, H, nc),
        in_specs=[slot(0), slot(1), slot(2), slot(3), tab, tab,
                  pl.BlockSpec((1, C, C), lambda b, h, c: (h, 0, 0)),
                  pl.BlockSpec((1, C, dh), lambda b, h, c: (h, 0, 0)),
                  pl.BlockSpec((1, C, dh), lambda b, h, c: (h, 0, 0)),
                  pl.BlockSpec((1, 8, dh), lambda b, h, c: (h, 0, 0)),
                  pl.BlockSpec((1, 1, dh, dh), lambda b, h, c: (b, h, 0, 0)),
                  pl.BlockSpec((1, dh), lambda b, h, c: (0, h))],
        out_specs=[pl.BlockSpec((1, C, dh), lambda b, h, c: (b, c, h)),
                   pl.BlockSpec((1, 1, dh, dh), lambda b, h, c: (b, h, 0, 0))],
        out_shape=[jax.ShapeDtypeStruct((B, L, BW), BF16),
                   jax.ShapeDtypeStruct((B, H, dh, dh), F32)],
        scratch_shapes=[pltpu.VMEM((dh, dh), F32)],
        compiler_params=_params("parallel", "parallel", "arbitrary"),
        name="retention",
    )(proj, proj, proj, proj, cos, sin, dec, qd, kd, gl, state0, gn_w.reshape(1, BW))


def t5_bucket(n):
    max_exact = REL_BUCKETS // 2
    nf = jnp.maximum(n, 1).astype(F32)
    large = max_exact + (jnp.log(nf / max_exact) / math.log(REL_MAX_DIST / max_exact) * (REL_BUCKETS - max_exact)).astype(jnp.int32)
    large = jnp.minimum(large, REL_BUCKETS - 1)
    return jnp.where(n < max_exact, n, large)


def _select_blocks(gate, n_elig, nb):
    col = lax.broadcasted_iota(jnp.int32, gate.shape, 1)
    rank = jnp.zeros(gate.shape, jnp.int32)
    for m in range(nb):
        gm = gate[:, m:m + 1]
        beats = (gm > gate) | ((gm == gate) & (m < col))
        rank = rank + jnp.where(beats, jnp.where(m < n_elig, 1, 0), 0)
    return (col < n_elig) & (rank < MOBA_TOPK)


def _moba_prompt_kernel(q_ref, k_ref, v_ref, tb_ref, o_ref, kbar_scr, kb_scr, vb_scr, *, nb, blk, scale):
    i = pl.program_id(2)

    @pl.when(i == 0)
    def _():
        for j in range(nb):
            kbar_scr[j:j + 1, :] = jnp.mean(k_ref[0, j * blk:(j + 1) * blk, :], axis=0, keepdims=True)
        kb_scr[...] = k_ref[0].astype(BF16)
        vb_scr[...] = v_ref[0].astype(BF16)

    q = q_ref[0]
    sel = _select_blocks(_fdot_nt(q, kbar_scr[...]), i, nb)
    qb = q.astype(BF16)
    r = lax.broadcasted_iota(jnp.int32, (blk, blk), 0)
    c = lax.broadcasted_iota(jnp.int32, (blk, blk), 1)
    causal = c <= r
    far_bias = tb_ref[0, 0:1, 0:1]

    def attend(nkb):
        s = _bdot_nt(qb, kb_scr[0:nkb * blk, :]) * scale
        parts = []
        for j in range(nkb):
            rel = jnp.full((blk, blk), i - j, jnp.int32)
            own = rel == 0
            bias = jnp.where(own, tb_ref[0, :, blk:], jnp.where(rel == 1, tb_ref[0, :, :blk], far_bias))
            ok = (own & causal) | (jnp.logical_not(own) & sel[:, j:j + 1])
            parts.append(jnp.where(ok, s[:, j * blk:(j + 1) * blk] + bias, -jnp.inf))
        s = jnp.concatenate(parts, axis=1)
        m = jnp.max(s, axis=-1, keepdims=True)
        p = jnp.exp(s - m)
        l = jnp.sum(p, axis=-1, keepdims=True)
        o_ref[0] = (_bdot(p, vb_scr[0:nkb * blk, :]) / l).astype(o_ref.dtype)

    if nb % 2 == 0:
        @pl.when(i < nb // 2)
        def _():
            attend(nb // 2)

        @pl.when(i >= nb // 2)
        def _():
            attend(nb)
    else:
        attend(nb)


def moba_prompt(proj, rel_bias, H):
    B, L, _ = proj.shape
    blk = MOBA_BLOCK
    dh = MOBA_HEAD_DIM
    nb = L // blk
    assert L % blk == 0 and blk > REL_MAX_DIST
    r = jnp.arange(blk)[:, None]
    c = jnp.arange(2 * blk)[None, :]
    dist = jnp.maximum(blk + r - c, 0)
    tb = rel_bias.astype(F32)[t5_bucket(dist)].transpose(2, 0, 1)
    return pl.pallas_call(
        functools.partial(_moba_prompt_kernel, nb=nb, blk=blk, scale=dh ** -0.5),
        grid=(B, H, nb),
        in_specs=[pl.BlockSpec((1, blk, dh), lambda b, h, i: (b, i, 4 * H + h)),
                  pl.BlockSpec((1, L, dh), lambda b, h, i: (b, 0, 5 * H + h)),
                  pl.BlockSpec((1, L, dh), lambda b, h, i: (b, 0, 6 * H + h)),
                  pl.BlockSpec((1, blk, 2 * blk), lambda b, h, i: (h, 0, 0))],
        out_specs=pl.BlockSpec((1, blk, dh), lambda b, h, i: (b, i, h)),
        out_shape=jax.ShapeDtypeStruct((B, L, H * dh), BF16),
        scratch_shapes=[pltpu.VMEM((nb, dh), F32), pltpu.VMEM((L, dh), BF16), pltpu.VMEM((L, dh), BF16)],
        compiler_params=_params("parallel", "parallel", "arbitrary"),
        name="moba_prompt",
    )(proj, proj, proj, tb)


def _kv_means_kernel(pt_ref, *refs, pps, ppb, inv_blk):
    k_refs = refs[:pps]
    o_ref = refs[pps]
    s = pl.program_id(1)

    @pl.when(s == 0)
    def _():
        o_ref[...] = jnp.zeros_like(o_ref)

    H = o_ref.shape[1]
    for i in range(0, pps, ppb):
        tot = jnp.sum(k_refs[i][0, 0], axis=0)
        for u in range(1, ppb):
            tot = tot + jnp.sum(k_refs[i + u][0, 0], axis=0)
        mean = tot * inv_blk
        n = s * (pps // ppb) + i // ppb
        for h in range(H):
            o_ref[0, h, pl.ds(n, 1), :] = mean[h:h + 1, :]


def kv_block_means(cache_k, layer, page_table):
    _, _, page, H, dh = cache_k.shape
    DB, n_pages = page_table.shape
    ppb = MOBA_BLOCK // page
    pps = KV_PAGES_PER_STEP
    assert MOBA_BLOCK % page == 0 and n_pages % pps == 0 and pps % ppb == 0 and n_pages // ppb <= LANES

    def kspec(i):
        return pl.BlockSpec((1, 1, page, H, dh), lambda b, s, pt: (layer, pt[b, s * pps + i], 0, 0, 0))

    return pl.pallas_call(
        functools.partial(_kv_means_kernel, pps=pps, ppb=ppb, inv_blk=1.0 / MOBA_BLOCK),
        grid_spec=pltpu.PrefetchScalarGridSpec(
            num_scalar_prefetch=1,
            grid=(DB, n_pages // pps),
            in_specs=[kspec(i) for i in range(pps)],
            out_specs=pl.BlockSpec((1, H, LANES, dh), lambda b, s, pt: (b, 0, 0, 0))),
        out_shape=jax.ShapeDtypeStruct((DB, H, LANES, dh), F32),
        compiler_params=_params("parallel", "arbitrary"),
        name="kv_block_means",
    )(page_table, *([cache_k] * pps))


def _moba_sample_kernel(pt_ref, q_ref, kn_ref, vn_ref, kbar_ref, bown_ref, blast_ref, bfar_ref, *refs,
                        pps, ppb, nsteps, nbp, H, scale):
    k_refs = refs[:pps]
    v_refs = refs[pps:2 * pps]
    o_ref = refs[2 * pps]
    q_scr, sel_scr, m_scr, l_scr, acc_scr = refs[2 * pps + 1:]
    s = pl.program_id(1)
    ds = q_ref.shape[1]
    dh = q_scr.shape[1]
    R = H * ds
    W = k_refs[0].shape[2]

    def head_major(ref):
        return jnp.concatenate([ref[0, :, h * dh:(h + 1) * dh] for h in range(H)], axis=0)

    @pl.when(s == 0)
    def _():
        q = head_major(q_ref)
        q_scr[...] = q
        gate = jnp.concatenate([_fdot_nt(q[h * ds:(h + 1) * ds, :], kbar_ref[0, h]) for h in range(H)], axis=0)
        sel_scr[...] = jnp.where(_select_blocks(gate, nbp, nbp), 1.0, 0.0)
        kn = head_major(kn_ref)
        vn = head_major(vn_ref)
        sc = _bdot_nt(q, kn) * scale + bown_ref[...]
        r = lax.broadcasted_iota(jnp.int32, (R, R), 0)
        c = lax.broadcasted_iota(jnp.int32, (R, R), 1)
        sc = jnp.where((c // ds == r // ds) & (c % ds <= r % ds), sc, -jnp.inf)
        m = jnp.max(sc, axis=-1, keepdims=True)
        p = jnp.exp(sc - m)
        m_scr[...] = m
        l_scr[...] = jnp.sum(p, axis=-1, keepdims=True)
        acc_scr[...] = _bdot(p, vn)

    @pl.when(s > 0)
    def _():
        qb = q_scr[...].astype(BF16)
        r = lax.broadcasted_iota(jnp.int32, (R, W), 0)
        c = lax.broadcasted_iota(jnp.int32, (R, W), 1)
        same_head = (c % H) == (r // ds)
        is_last_step = jnp.full((R, W), s, jnp.int32) == nsteps - 1
        bcol = lax.broadcasted_iota(jnp.int32, sel_scr.shape, 1)
        sel = sel_scr[...]
        far = bfar_ref[:, 0:1]
        m, l, acc = m_scr[...], l_scr[...], acc_scr[...]
        for i in range(pps):
            n = ((s - 1) * pps + i) // ppb
            picked = jnp.sum(jnp.where(bcol == n, sel, 0.0), axis=1, keepdims=True) > 0.0
            bias = jnp.where(is_last_step, blast_ref[...], far) if i == pps - 1 else far
            sc = _bdot_nt(qb, k_refs[i][0, 0]) * scale + bias
            sc = jnp.where(same_head & picked, sc, -jnp.inf)
            m_new = jnp.maximum(m, jnp.max(sc, axis=-1, keepdims=True))
            a = jnp.exp(m - m_new)
            p = jnp.exp(sc - m_new)
            l = a * l + jnp.sum(p, axis=-1, keepdims=True)
            acc = a * acc + _bdot(p, v_refs[i][0, 0])
            m = m_new
        m_scr[...] = m
        l_scr[...] = l
        acc_scr[...] = acc

        @pl.when(s == nsteps - 1)
        def _():
            o = acc / l
            for h in range(H):
                o_ref[0, :, h * dh:(h + 1) * dh] = o[h * ds:(h + 1) * ds, :].astype(o_ref.dtype)


def moba_sample(proj, cache_k, cache_v, layer, page_table, rel_bias):
    DB, DS, _ = proj.shape
    _, n_pool, page, H, dh = cache_k.shape
    n_pages = page_table.shape[1]
    past_len = n_pages * page
    ppb = MOBA_BLOCK // page
    pps = KV_PAGES_PER_STEP
    nbp = past_len // MOBA_BLOCK
    assert past_len % MOBA_BLOCK == 0 and DS <= MOBA_BLOCK and page >= REL_MAX_DIST and DS <= page
    assert n_pages % pps == 0 and pps % ppb == 0 and nbp <= LANES
    nsteps = 1 + n_pages // pps
    R = H * DS
    BW = H * dh
    kbar = kv_block_means(cache_k, layer, page_table)
    rb = rel_bias.astype(F32)
    hq_h = jnp.repeat(jnp.arange(H), DS)
    hq_q = jnp.tile(jnp.arange(DS), H)
    own_t = jnp.tile(jnp.arange(DS), H)
    b_own = rb[t5_bucket(jnp.maximum(hq_q[:, None] - own_t[None, :], 0)), hq_h[:, None]]
    last_t = jnp.repeat(jnp.arange(page), H)
    b_last = rb[t5_bucket(page + hq_q[:, None] - last_t[None, :]), hq_h[:, None]]
    b_far = jnp.broadcast_to(rb[REL_BUCKETS - 1, hq_h][:, None], (R, LANES))
    ck = cache_k.reshape(cache_k.shape[0], n_pool, page * H, dh)
    cv = cache_v.reshape(cache_v.shape[0], n_pool, page * H, dh)

    def pspec(i):
        return pl.BlockSpec((1, 1, page * H, dh),
                            lambda b, s, pt: (layer, pt[b, jnp.maximum(s - 1, 0) * pps + i], 0, 0))

    def slot(sl):
        return pl.BlockSpec((1, DS, BW), lambda b, s, pt: (b, 0, sl))

    const = lambda a: pl.BlockSpec(a.shape, lambda b, s, pt: (0,) * a.ndim)
    return pl.pallas_call(
        functools.partial(_moba_sample_kernel, pps=pps, ppb=ppb, nsteps=nsteps, nbp=nbp, H=H, scale=dh ** -0.5),
        grid_spec=pltpu.PrefetchScalarGridSpec(
            num_scalar_prefetch=1,
            grid=(DB, nsteps),
            in_specs=[slot(4), slot(5), slot(6),
                      pl.BlockSpec((1, H, LANES, dh), lambda b, s, pt: (b, 0, 0, 0)),
                      const(b_own), const(b_last), const(b_far)]
                     + [pspec(i) for i in range(pps)] + [pspec(i) for i in range(pps)],
            out_specs=pl.BlockSpec((1, DS, BW), lambda b, s, pt: (b, 0, 0)),
            scratch_shapes=[pltpu.VMEM((R, dh), F32), pltpu.VMEM((R, LANES), F32),
                            pltpu.VMEM((R, 1), F32), pltpu.VMEM((R, 1), F32), pltpu.VMEM((R, dh), F32)]),
        out_shape=jax.ShapeDtypeStruct((DB, DS, BW), BF16),
        compiler_params=_params("parallel", "arbitrary"),
        name="moba_sample",
    )(page_table, proj, proj, proj, kbar, b_own, b_last, b_far, *([ck] * pps), *([cv] * pps))


def _conv_kernel(p7, p8, p9, p10, p11, scb, cfb, sw, cw, cb, nw, nbias, oc_ref, od_ref, scn_ref, cfn_ref,
                 extc, extd, ybuf, *, tl, nt, ws, wd):
    t = pl.program_id(1)
    P = CONV_PAD
    W = ybuf.shape[1]

    @pl.when(t == 0)
    def _():
        extc[pl.ds(P - (ws - 1), ws - 1), :] = scb[0]
        extd[pl.ds(P - (wd - 1), wd - 1), :] = cfb[0]

    extc[pl.ds(P, tl), :] = p8[0] * p9[0]
    extd[pl.ds(P, tl), :] = p10[0] * jax.nn.sigmoid(p11[0])

    yc = sw[0:1, :] * extc[pl.ds(P - (ws - 1), tl), :]
    for j in range(1, ws):
        yc = yc + sw[j:j + 1, :] * extc[pl.ds(P - (ws - 1) + j, tl), :]
    oc_ref[0] = (p7[0] * yc).astype(oc_ref.dtype)

    for c0 in range(0, W, LANES):
        cols = slice(c0, c0 + LANES)
        acc = cw[0:1, cols] * extd[pl.ds(P - (wd - 1), tl), cols]
        for j in range(1, wd):
            acc = acc + cw[j:j + 1, cols] * extd[pl.ds(P - (wd - 1) + j, tl), cols]
        ybuf[:, cols] = acc + cb[:, cols]
    y = ybuf[...]
    yc_ = y - jnp.mean(y, axis=-1, keepdims=True)
    yn = yc_ * lax.rsqrt(jnp.mean(yc_ * yc_, axis=-1, keepdims=True) + EPS) * nw[...] + nbias[...]
    od_ref[0] = (yn * jax.nn.sigmoid(yn)).astype(od_ref.dtype)

    newc = extc[pl.ds(P + tl - (ws - 1), ws - 1), :]
    newd = extd[pl.ds(P + tl - (wd - 1), wd - 1), :]
    extc[pl.ds(P - (ws - 1), ws - 1), :] = newc
    extd[pl.ds(P - (wd - 1), wd - 1), :] = newd

    @pl.when(t == nt - 1)
    def _():
        scn_ref[0] = newc
        cfn_ref[0] = newd


def conv_branches(proj, sc_buf, cf_buf, sconv_w, cconv_w, cconv_b, cnorm_w, cnorm_b):
    B, L, _ = proj.shape
    ws, BW = sconv_w.shape
    wd = cconv_w.shape[0]
    assert wd - 1 <= CONV_PAD
    tl = _pick(L, (128, 64, 32, 16, 8))
    nt = L // tl

    def slot(s):
        return pl.BlockSpec((1, tl, BW), lambda b, t: (b, t, s))

    full = lambda a: pl.BlockSpec(a.shape, lambda b, t: (0,) * a.ndim)
    cb2, nw2, nb2 = cconv_b.reshape(1, BW), cnorm_w.reshape(1, BW), cnorm_b.reshape(1, BW)
    ospec = pl.BlockSpec((1, tl, BW), lambda b, t: (b, t, 0))
    return pl.pallas_call(
        functools.partial(_conv_kernel, tl=tl, nt=nt, ws=ws, wd=wd),
        grid=(B, nt),
        in_specs=[slot(7), slot(8), slot(9), slot(10), slot(11),
                  pl.BlockSpec((1, ws - 1, BW), lambda b, t: (b, 0, 0)),
                  pl.BlockSpec((1, wd - 1, BW), lambda b, t: (b, 0, 0)),
                  full(sconv_w), full(cconv_w), full(cb2), full(nw2), full(nb2)],
        out_specs=[ospec, ospec,
                   pl.BlockSpec((1, ws - 1, BW), lambda b, t: (b, 0, 0)),
                   pl.BlockSpec((1, wd - 1, BW), lambda b, t: (b, 0, 0))],
        out_shape=[jax.ShapeDtypeStruct((B, L, BW), BF16), jax.ShapeDtypeStruct((B, L, BW), BF16),
                   jax.ShapeDtypeStruct((B, ws - 1, BW), F32), jax.ShapeDtypeStruct((B, wd - 1, BW), F32)],
        scratch_shapes=[pltpu.VMEM((CONV_PAD + tl, BW), F32), pltpu.VMEM((CONV_PAD + tl, BW), F32),
                        pltpu.VMEM((tl, BW), F32)],
        compiler_params=_params("parallel", "arbitrary"),
        name="conv_branches",
    )(proj, proj, proj, proj, proj, sc_buf, cf_buf, sconv_w, cconv_w, cb2, nw2, nb2)


def _gate_merge_kernel(h_ref, *refs):
    br = refs[0:4]
    wg = refs[4:8]
    wb = refs[8:12]
    bg = refs[12:16]
    o_ref = refs[16]
    h = h_ref[...]
    acc = None
    for n in range(N_BRANCH):
        gate = jax.nn.sigmoid(_bdot(h, wg[n][...]) + bg[n][0])
        term = gate * _bdot(br[n][...], wb[n][...])
        acc = term if acc is None else acc + term
    o_ref[...] = acc.astype(o_ref.dtype)


def gate_merge(h, branches, w_gate, b_gate, w_branch, layer):
    M, D = h.shape
    BW = branches[0].shape[1]
    tm = _pick(M, (512, 256, 128, 64))
    tn = 256
    nj = D // tn
    part = functools.partial
    in_specs = [pl.BlockSpec((tm, D), lambda i, j: (i, 0))]
    in_specs += [pl.BlockSpec((tm, BW), lambda i, j: (i, 0)) for _ in range(N_BRANCH)]
    in_specs += [pl.BlockSpec((None, D, tn), part(lambda i, j, n: (layer, 0, n * nj + j), n=n)) for n in range(N_BRANCH)]
    in_specs += [pl.BlockSpec((None, None, BW, tn), part(lambda i, j, n: (layer, n, 0, j), n=n)) for n in range(N_BRANCH)]
    in_specs += [pl.BlockSpec((1, 1, tn), part(lambda i, j, n: (n, 0, j), n=n)) for n in range(N_BRANCH)]
    bg3 = b_gate.reshape(N_BRANCH, 1, D)
    return pl.pallas_call(
        _gate_merge_kernel,
        grid=(M // tm, nj),
        in_specs=in_specs,
        out_specs=pl.BlockSpec((tm, tn), lambda i, j: (i, j)),
        out_shape=jax.ShapeDtypeStruct((M, D), BF16),
        compiler_params=_params("parallel", "parallel"),
        name="gate_merge",
    )(h, *branches, *([w_gate] * N_BRANCH), *([w_branch] * N_BRANCH), *([bg3] * N_BRANCH))


def _ffn_in_kernel(h_ref, wa_ref, wb_ref, o_ref):
    h = h_ref[...]
    a = _bdot(h, wa_ref[...])
    b = _bdot(h, wb_ref[...])
    o_ref[...] = (a * jax.nn.sigmoid(a) * b).astype(o_ref.dtype)


def ffn_in(h, w, layer, *, tm=None, single_buffer_h=False):
    M, D = h.shape
    F = w.shape[2] // 2
    tm = tm or _pick(M, (1024, 512, 256, 128, 64))
    tn = 256
    nj = F // tn
    hspec = (pl.BlockSpec((tm, D), lambda i, j: (i, 0), pipeline_mode=pl.Buffered(1)) if single_buffer_h
             else pl.BlockSpec((tm, D), lambda i, j: (i, 0)))
    return pl.pallas_call(
        _ffn_in_kernel,
        grid=(M // tm, nj),
        in_specs=[hspec,
                  pl.BlockSpec((None, D, tn), lambda i, j: (layer, 0, j)),
                  pl.BlockSpec((None, D, tn), lambda i, j: (layer, 0, nj + j))],
        out_specs=pl.BlockSpec((tm, tn), lambda i, j: (i, j)),
        out_shape=jax.ShapeDtypeStruct((M, F), BF16),
        compiler_params=_params("parallel", "parallel"),
        name="ffn_in",
    )(h, w, w)


def decoder_layer(x, h, pos, past, ret_state, sc_buf, cf_buf, mod, row0, nxt, layer, lw, ffn_in_f32):
    (norm_g, w_in, w_gate, b_gate, ret_gn_w, sconv_w, cconv_w, cconv_b, cnorm_w, cnorm_b,
     w_branch, w_o, w_ffn_in, w_ffn_out, rel_bias) = lw
    B, L, D = x.shape
    BW = D // N_BRANCH
    H = BW // MOBA_HEAD_DIM
    M = B * L
    F = w_ffn_out.shape[1]
    h2d = h.reshape(M, D)
    proj = matmul(h2d, w_in, layer).reshape(B, L, N_IN_SLOTS * BW)
    o_a, ret_new = retention(proj, pos, ret_state, ret_gn_w)
    k_b = proj[:, :, 5 * BW:6 * BW].reshape(B, L, H, MOBA_HEAD_DIM)
    v_b = proj[:, :, 6 * BW:7 * BW].reshape(B, L, H, MOBA_HEAD_DIM)
    if past is None:
        o_b = moba_prompt(proj, rel_bias, H)
    else:
        cache_k, cache_v, page_table = past
        o_b = moba_sample(proj, cache_k, cache_v, layer, page_table, rel_bias)
    o_c, o_d, sc_new, cf_new = conv_branches(proj, sc_buf, cf_buf, sconv_w, cconv_w, cconv_b, cnorm_w, cnorm_b)
    branches = [a.reshape(M, BW) for a in (o_a, o_b, o_c, o_d)]
    merged = gate_merge(h2d, branches, w_gate, b_gate, w_branch, layer)
    y = matmul(merged, w_o, layer).reshape(B, L, D)
    x, h = resid_norm(x, y, norm_g[1], mod, row0, 2, nxt=(norm_g[2], mod, 4, 3))
    if ffn_in_f32 is not None and M % 2048 == 0:
        act = ffn_in(h.reshape(M, D), ffn_in_f32, layer, tm=2048, single_buffer_h=True)
    else:
        act = ffn_in(h.reshape(M, D), w_ffn_in, layer)
    y2 = matmul(act, w_ffn_out, layer, tn=512, tk=F // 2).reshape(B, L, D)
    x, h = resid_norm(x, y2, norm_g[3], mod, row0, 5, nxt=nxt)
    return x, h, (k_b, v_b, ret_new, sc_new, cf_new)


def kernel(x_prompt, x_sample, c_prompt, c_sample, cache_k, cache_v, state_ret, state_sconv, state_cconv, page_table, w_ada, b_ada, norm_g, w_in, w_gate, b_gate, ret_gn_w, sconv_w, cconv_w, cconv_b, cnorm_w, cnorm_b, w_branch, w_o, w_ffn_in, w_ffn_out, rel_bias):
    B, S, D = x_prompt.shape
    DB, DS, _ = x_sample.shape
    depth = w_in.shape[0]
    BW = D // N_BRANCH
    H = BW // MOBA_HEAD_DIM
    dh_ret = BW // RET_HEADS
    page = cache_k.shape[2]
    past_len = page_table.shape[1] * page
    pos_p = jnp.arange(S, dtype=jnp.int32)
    pos_s = past_len + jnp.arange(DS, dtype=jnp.int32)

    rows = B + DB
    rows_pad = -(-rows // 8) * 8
    c_all = jnp.concatenate([c_prompt, c_sample, jnp.zeros((rows_pad - rows, D), F32)], axis=0)
    mods = ada_mod(c_all, w_ada, b_ada).reshape(depth, rows_pad * N_MOD, 1, D)

    w_in_b, w_gate_b, w_branch_b, w_o_b = (w.astype(BF16) for w in (w_in, w_gate, w_branch, w_o))
    w_ffn_in_b, w_ffn_out_b = w_ffn_in.astype(BF16), w_ffn_out.astype(BF16)

    xp, xs = x_prompt, x_sample
    hp = norm_mod(xp, norm_g[0, 0], mods[0], 0, (1, 0))
    hs = norm_mod(xs, norm_g[0, 0], mods[0], B, (1, 0))
    zeros_ret = jnp.zeros((B, RET_HEADS, dh_ret, dh_ret), F32)
    zeros_sc = jnp.zeros((B, sconv_w.shape[1] - 1, BW), F32)
    zeros_cf = jnp.zeros((B, cconv_w.shape[1] - 1, BW), F32)
    outs_p, outs_s = [], []
    for l in range(depth):
        lw = (norm_g[l], w_in_b, w_gate_b, b_gate[l], ret_gn_w[l], sconv_w[l], cconv_w[l], cconv_b[l],
              cnorm_w[l], cnorm_b[l], w_branch_b, w_o_b, w_ffn_in_b, w_ffn_out_b, rel_bias)
        nxt = (norm_g[l + 1, 0], mods[l + 1], 1, 0) if l + 1 < depth else None
        f32w = w_ffn_in if l == depth - 1 else None
        xp, hp, op = decoder_layer(xp, hp, pos_p, None, zeros_ret, zeros_sc, zeros_cf, mods[l], 0, nxt, l, lw, f32w)
        xs, hs, os_ = decoder_layer(xs, hs, pos_s, (cache_k, cache_v, page_table), state_ret[l], state_sconv[l],
                                    state_cconv[l], mods[l], B, nxt, l, lw, None)
        outs_p.append(op)
        outs_s.append(os_)
    stack = lambda outs, i: jnp.stack([o[i] for o in outs])
    new_k_prompt = stack(outs_p, 0).reshape(depth, B, S // page, page, H, MOBA_HEAD_DIM)
    new_v_prompt = stack(outs_p, 1).reshape(depth, B, S // page, page, H, MOBA_HEAD_DIM)
    return (xp, xs, new_k_prompt, new_v_prompt, stack(outs_s, 0), stack(outs_s, 1),
            stack(outs_p, 2), stack(outs_s, 2), stack(outs_p, 3), stack(outs_s, 3),
            stack(outs_p, 4), stack(outs_s, 4))
```

```python
import functools
import math

import jax
import jax.numpy as jnp
from jax import lax
from jax.experimental import pallas as pl
from jax.experimental.pallas import tpu as pltpu

F32 = jnp.float32
BF16 = jnp.bfloat16

N_BRANCH = 4
N_IN_SLOTS = 12
RET_HEADS = 4
RET_CHUNK = 128
ROPE_BASE = 10000.0
MOBA_HEAD_DIM = 128
MOBA_BLOCK = 256
MOBA_TOPK = 3
REL_BUCKETS = 32
REL_MAX_DIST = 128
N_MOD = 6
EPS = 1e-6
GN_EPS = 1e-5
CONV_PAD = 32
LANES = 128
KV_PAGES_PER_STEP = 8

VMEM_LIMIT_BYTES = 56 * 1024 * 1024


def _params(*sem):
    return pltpu.CompilerParams(dimension_semantics=sem, vmem_limit_bytes=VMEM_LIMIT_BYTES)


def _pick(dim, prefs):
    for p in prefs:
        if dim % p == 0:
            return p
    return dim


def _bdot(a, b):
    return jnp.dot(a.astype(BF16), b.astype(BF16), preferred_element_type=F32)


def _bdot_nt(a, b):
    return lax.dot_general(a.astype(BF16), b.astype(BF16), (((1,), (1,)), ((), ())), preferred_element_type=F32)


def _bdot_tn(a, b):
    return lax.dot_general(a.astype(BF16), b.astype(BF16), (((0,), (0,)), ((), ())), preferred_element_type=F32)


def _fdot_nt(a, b):
    return lax.dot_general(a, b, (((1,), (1,)), ((), ())), precision=lax.Precision.HIGHEST, preferred_element_type=F32)


def _resident(shape, imap):
    return pl.BlockSpec(shape, imap, pipeline_mode=pl.Buffered(1))


def _mm_kernel(xp_ref, xs_ref, w_ref, op_ref, os_ref, *, nk):
    i = pl.program_id(0)
    w = w_ref[...].astype(BF16)
    part = jnp.dot(xp_ref[...], w, preferred_element_type=F32)
    if nk == 1:
        op_ref[...] = part

        @pl.when(i == 0)
        def _():
            os_ref[0] = jnp.dot(xs_ref[...], w, preferred_element_type=F32)
    else:
        k = pl.program_id(2)

        @pl.when(k == 0)
        def _():
            op_ref[...] = part

        @pl.when(k > 0)
        def _():
            op_ref[...] += part

        @pl.when((i == 0) & (k == 0))
        def _():
            os_ref[0] = jnp.dot(xs_ref[...], w, preferred_element_type=F32)

        @pl.when((i == 0) & (k > 0))
        def _():
            os_ref[0] += jnp.dot(xs_ref[...], w, preferred_element_type=F32)


def _unblock(o_s):
    nj, Ms, tn = o_s.shape
    return o_s.transpose(1, 0, 2).reshape(Ms, nj * tn)


def matmul(xp, xs, w, layer, *, tm, tn, tk=None):
    Mp, K = xp.shape
    Ms = xs.shape[0]
    N = w.shape[2]
    tm = min(tm, Mp)
    tk = tk or K
    nk = K // tk
    nj = N // tn
    assert Mp % tm == 0 and N % tn == 0 and K % tk == 0
    xp_spec = (_resident((tm, tk), lambda i, j, k: (i, k)) if nk == 1
               else pl.BlockSpec((tm, tk), lambda i, j, k: (i, k)))
    o_p, o_s = pl.pallas_call(
        functools.partial(_mm_kernel, nk=nk),
        grid=(Mp // tm, nj, nk),
        in_specs=[xp_spec,
                  pl.BlockSpec((Ms, tk), lambda i, j, k: (0, k)),
                  pl.BlockSpec((None, tk, tn), lambda i, j, k: (layer, k, j))],
        out_specs=[pl.BlockSpec((tm, tn), lambda i, j, k: (i, j)),
                   pl.BlockSpec((1, Ms, tn), lambda i, j, k: (jnp.where(i == 0, j, nj - 1), 0, 0))],
        out_shape=[jax.ShapeDtypeStruct((Mp, N), F32), jax.ShapeDtypeStruct((nj, Ms, tn), F32)],
        compiler_params=_params("arbitrary", "arbitrary", "arbitrary"),
        name="matmul",
    )(xp, xs, w)
    return o_p, _unblock(o_s)


def _gate_merge_kernel(hp_ref, hs_ref, *refs):
    brp = refs[0:4]
    brs = refs[4:8]
    wg_ref, wb_ref, bg_ref, op_ref, os_ref, accp, accs = refs[8:]
    i = pl.program_id(0)
    n = pl.program_id(2)
    wg = wg_ref[...].astype(BF16)
    wb = wb_ref[...].astype(BF16)
    bg = bg_ref[0]

    def term(h_ref, br_ref):
        gate = jax.nn.sigmoid(jnp.dot(h_ref[...], wg, preferred_element_type=F32) + bg)
        return gate * jnp.dot(br_ref[...], wb, preferred_element_type=F32)

    for b in range(N_BRANCH):
        @pl.when(n == b)
        def _(b=b):
            t = term(hp_ref, brp[b])
            if b == 0:
                accp[...] = t
            elif b < N_BRANCH - 1:
                accp[...] += t
            else:
                op_ref[...] = (accp[...] + t).astype(op_ref.dtype)

        @pl.when((n == b) & (i == 0))
        def _(b=b):
            t = term(hs_ref, brs[b])
            if b == 0:
                accs[...] = t
            elif b < N_BRANCH - 1:
                accs[...] += t
            else:
                os_ref[0] = (accs[...] + t).astype(os_ref.dtype)


def gate_merge(hp, hs, branches_p, branches_s, w_gate, b_gate, w_branch, layer, *, tm=1024, tn=512):
    Mp, D = hp.shape
    Ms = hs.shape[0]
    BW = branches_p[0].shape[1]
    tm = min(tm, Mp)
    nj = D // tn
    assert Mp % tm == 0 and D % tn == 0
    bg3 = b_gate.reshape(N_BRANCH, 1, D)
    in_specs = [_resident((tm, D), lambda i, j, n: (i, 0)),
                pl.BlockSpec((Ms, D), lambda i, j, n: (0, 0))]
    in_specs += [_resident((tm, BW), lambda i, j, n: (i, 0)) for _ in range(N_BRANCH)]
    in_specs += [pl.BlockSpec((Ms, BW), lambda i, j, n: (0, 0)) for _ in range(N_BRANCH)]
    in_specs += [pl.BlockSpec((None, D, tn), lambda i, j, n: (layer, 0, n * nj + j)),
                 pl.BlockSpec((None, None, BW, tn), lambda i, j, n: (layer, n, 0, j)),
                 pl.BlockSpec((1, 1, tn), lambda i, j, n: (n, 0, j))]
    o_p, o_s = pl.pallas_call(
        _gate_merge_kernel,
        grid=(Mp // tm, nj, N_BRANCH),
        in_specs=in_specs,
        out_specs=[pl.BlockSpec((tm, tn), lambda i, j, n: (i, j)),
                   pl.BlockSpec((1, Ms, tn), lambda i, j, n: (jnp.where(i == 0, j, nj - 1), 0, 0))],
        out_shape=[jax.ShapeDtypeStruct((Mp, D), BF16), jax.ShapeDtypeStruct((nj, Ms, tn), BF16)],
        scratch_shapes=[pltpu.VMEM((tm, tn), F32), pltpu.VMEM((Ms, tn), F32)],
        compiler_params=_params("arbitrary", "arbitrary", "arbitrary"),
        name="gate_merge",
    )(hp, hs, *branches_p, *branches_s, w_gate, w_branch, bg3)
    return o_p, _unblock(o_s)


def _ffn_in_kernel(hp_ref, hs_ref, wa_ref, wb_ref, op_ref, os_ref):
    wa = wa_ref[...].astype(BF16)
    wb = wb_ref[...].astype(BF16)

    def act(h_ref):
        a = jnp.dot(h_ref[...], wa, preferred_element_type=F32)
        b = jnp.dot(h_ref[...], wb, preferred_element_type=F32)
        return a * jax.nn.sigmoid(a) * b

    op_ref[...] = act(hp_ref).astype(op_ref.dtype)

    @pl.when(pl.program_id(0) == 0)
    def _():
        os_ref[0] = act(hs_ref).astype(os_ref.dtype)


def ffn_in(hp, hs, w, layer, *, tm=2048, tn=256):
    Mp, D = hp.shape
    Ms = hs.shape[0]
    F = w.shape[2] // 2
    tm = min(tm, Mp)
    nj = F // tn
    assert Mp % tm == 0 and F % tn == 0
    o_p, o_s = pl.pallas_call(
        _ffn_in_kernel,
        grid=(Mp // tm, nj),
        in_specs=[_resident((tm, D), lambda i, j: (i, 0)),
                  pl.BlockSpec((Ms, D), lambda i, j: (0, 0)),
                  pl.BlockSpec((None, D, tn), lambda i, j: (layer, 0, j)),
                  pl.BlockSpec((None, D, tn), lambda i, j: (layer, 0, nj + j))],
        out_specs=[pl.BlockSpec((tm, tn), lambda i, j: (i, j)),
                   pl.BlockSpec((1, Ms, tn), lambda i, j: (jnp.where(i == 0, j, nj - 1), 0, 0))],
        out_shape=[jax.ShapeDtypeStruct((Mp, F), BF16), jax.ShapeDtypeStruct((nj, Ms, tn), BF16)],
        compiler_params=_params("arbitrary", "arbitrary"),
        name="ffn_in",
    )(hp, hs, w, w)
    return o_p, _unblock(o_s)


def _ada_kernel(c_ref, w_ref, b_ref, o_ref):
    c = c_ref[...]
    x = c * jax.nn.sigmoid(c)
    o_ref[0] = _bdot(x, w_ref[0]) + b_ref[0]


def ada_mod(c_all, w_ada, b_ada):
    depth, D, N = w_ada.shape
    R = c_all.shape[0]
    tn = 512
    return pl.pallas_call(
        _ada_kernel,
        grid=(depth, N // tn),
        in_specs=[pl.BlockSpec((R, D), lambda l, j: (0, 0)),
                  pl.BlockSpec((1, D, tn), lambda l, j: (l, 0, j)),
                  pl.BlockSpec((1, 1, tn), lambda l, j: (l, 0, j))],
        out_specs=pl.BlockSpec((1, R, tn), lambda l, j: (l, 0, j)),
        out_shape=jax.ShapeDtypeStruct((depth, R, N), F32),
        compiler_params=_params("parallel", "parallel"),
        name="ada_mod",
    )(c_all, w_ada, b_ada.reshape(depth, 1, N))


def _rms(x, g):
    return x * lax.rsqrt(jnp.mean(x * x, axis=-1, keepdims=True) + EPS) * g


def _norm_mod_kernel(x_ref, g_ref, sc_ref, sh_ref, h_ref):
    h = _rms(x_ref[0], g_ref[...]) * (1.0 + sc_ref[0]) + sh_ref[0]
    h_ref[0] = h.astype(h_ref.dtype)


def _mod_spec(D, row0, slot):
    return pl.BlockSpec((1, 1, D), lambda b, t: ((row0 + b) * N_MOD + slot, 0, 0))


def norm_mod(x, g, mod, row0, slots):
    B, L, D = x.shape
    tl = _pick(L, (256, 128, 64, 32, 16, 8))
    return pl.pallas_call(
        _norm_mod_kernel,
        grid=(B, L // tl),
        in_specs=[pl.BlockSpec((1, tl, D), lambda b, t: (b, t, 0)),
                  pl.BlockSpec((1, D), lambda b, t: (0, 0)),
                  _mod_spec(D, row0, slots[0]), _mod_spec(D, row0, slots[1])],
        out_specs=pl.BlockSpec((1, tl, D), lambda b, t: (b, t, 0)),
        out_shape=jax.ShapeDtypeStruct((B, L, D), BF16),
        compiler_params=_params("parallel", "parallel"),
        name="norm_mod",
    )(x, g.reshape(1, D), mod, mod)


def _resid_kernel(x_ref, y_ref, ga_ref, gate_ref, *rest, emit_h):
    xn = x_ref[0] + gate_ref[0] * _rms(y_ref[0], ga_ref[...])
    if emit_h:
        gb_ref, sc_ref, sh_ref, xo_ref, h_ref = rest
        xo_ref[0] = xn
        h_ref[0] = (_rms(xn, gb_ref[...]) * (1.0 + sc_ref[0]) + sh_ref[0]).astype(h_ref.dtype)
    else:
        (xo_ref,) = rest
        xo_ref[0] = xn


def resid_norm(x, y, g_a, mod, row0, gate_slot, nxt=None):
    B, L, D = x.shape
    tl = _pick(L, (128, 64, 32, 16, 8))
    xspec = pl.BlockSpec((1, tl, D), lambda b, t: (b, t, 0))
    gspec = pl.BlockSpec((1, D), lambda b, t: (0, 0))
    in_specs = [xspec, xspec, gspec, _mod_spec(D, row0, gate_slot)]
    args = [x, y, g_a.reshape(1, D), mod]
    out_specs = [xspec]
    out_shape = [jax.ShapeDtypeStruct((B, L, D), F32)]
    if nxt is not None:
        g_b, mod_n, sc_slot, sh_slot = nxt
        in_specs += [gspec, _mod_spec(D, row0, sc_slot), _mod_spec(D, row0, sh_slot)]
        args += [g_b.reshape(1, D), mod_n, mod_n]
        out_specs.append(xspec)
        out_shape.append(jax.ShapeDtypeStruct((B, L, D), BF16))
    res = pl.pallas_call(
        functools.partial(_resid_kernel, emit_h=nxt is not None),
        grid=(B, L // tl),
        in_specs=in_specs, out_specs=out_specs, out_shape=out_shape,
        compiler_params=_params("parallel", "parallel"),
        name="resid_norm",
    )(*args)
    return (res[0], res[1]) if nxt is not None else (res[0], None)


def _ret_kernel(q_ref, k_ref, v_ref, g_ref, cos_ref, sin_ref, dec_ref, qd_ref, kd_ref, gl_ref, s0_ref, gnw_ref,
                o_ref, so_ref, s_scr, *, nc, dh):
    c = pl.program_id(2)

    @pl.when(c == 0)
    def _():
        s_scr[...] = s0_ref[0, 0]

    half = dh // 2
    cos = cos_ref[...]
    sin = sin_ref[...]

    def rot(x):
        x1 = x[:, :half]
        x2 = x[:, half:]
        return jnp.concatenate([x1 * cos - x2 * sin, x1 * sin + x2 * cos], axis=-1)

    q = rot(q_ref[0])
    k = rot(k_ref[0]) * (dh ** -0.5)
    v = v_ref[0]
    S = s_scr[...]
    scores = _bdot_nt(q, k) * dec_ref[0]
    o = _bdot(scores, v) + _bdot(q * qd_ref[0], S)
    S_new = gl_ref[0, 0:1, :] * S + _bdot_tn(k * kd_ref[0], v)
    s_scr[...] = S_new

    @pl.when(c == nc - 1)
    def _():
        so_ref[0, 0] = S_new

    oc = o - jnp.mean(o, axis=-1, keepdims=True)
    on = oc * lax.rsqrt(jnp.mean(oc * oc, axis=-1, keepdims=True) + GN_EPS)
    g = g_ref[0]
    o_ref[0] = (on * gnw_ref[...] * (g * jax.nn.sigmoid(g))).astype(o_ref.dtype)


def retention(proj, pos, state0, gn_w):
    B, L, _ = proj.shape
    H = RET_HEADS
    BW = gn_w.shape[0]
    dh = BW // H
    half = dh // 2
    C = RET_CHUNK if L % RET_CHUNK == 0 else L
    nc = L // C
    inv = jnp.exp(-math.log(ROPE_BASE) * jnp.arange(half, dtype=F32) / half)
    ang = pos.astype(F32)[:, None] * inv[None, :]
    cos, sin = jnp.cos(ang), jnp.sin(ang)
    log_g = jnp.log1p(-jnp.exp2(-5.0 - jnp.arange(H, dtype=F32)))
    idx = jnp.arange(C, dtype=F32)
    diff = idx[:, None] - idx[None, :]
    causal = diff >= 0
    dec = jnp.where(causal[None], jnp.exp(log_g[:, None, None] * jnp.where(causal, diff, 0.0)[None]), 0.0)
    qd = jnp.broadcast_to(jnp.exp(log_g[:, None] * (idx[None, :] + 1.0))[:, :, None], (H, C, dh))
    kd = jnp.broadcast_to(jnp.exp(log_g[:, None] * ((C - 1.0) - idx[None, :]))[:, :, None], (H, C, dh))
    gl = jnp.broadcast_to(jnp.exp(log_g * C)[:, None, None], (H, 8, dh))
    hb = BW // dh

    def slot(s):
        return pl.BlockSpec((1, C, dh), lambda b, h, c: (b, c, s * hb + h))

    tab = pl.BlockSpec((C, half), lambda b, h, c: (c, 0))
    return pl.pallas_call(
        functools.partial(_ret_kernel, nc=nc, dh=dh),
        grid=(B, H, nc),
        in_specs=[slot(0), slot(1), slot(2), slot(3), tab, tab,
                  pl.BlockSpec((1, C, C), lambda b, h, c: (h, 0, 0)),
                  pl.BlockSpec((1, C, dh), lambda b, h, c: (h, 0, 0)),
                  pl.BlockSpec((1, C, dh), lambda b, h, c: (h, 0, 0)),
                  pl.BlockSpec((1, 8, dh), lambda b, h, c: (h, 0, 0)),
                  pl.BlockSpec((1, 1, dh, dh), lambda b, h, c: (b, h, 0, 0)),
                  pl.BlockSpec((1, dh), lambda b, h, c: (0, h))],
        out_specs=[pl.BlockSpec((1, C, dh), lambda b, h, c: (b, c, h)),
                   pl.BlockSpec((1, 1, dh, dh), lambda b, h, c: (b, h, 0, 0))],
        out_shape=[jax.ShapeDtypeStruct((B, L, BW), BF16),
                   jax.ShapeDtypeStruct((B, H, dh, dh), F32)],
        scratch_shapes=[pltpu.VMEM((dh, dh), F32)],
        compiler_params=_params("parallel", "parallel", "arbitrary"),
        name="retention",
    )(proj, proj, proj, proj, cos, sin, dec, qd, kd, gl, state0, gn_w.reshape(1, BW))


def t5_bucket(n):
    max_exact = REL_BUCKETS // 2
    nf = jnp.maximum(n, 1).astype(F32)
    large = max_exact + (jnp.log(nf / max_exact) / math.log(REL_MAX_DIST / max_exact) * (REL_BUCKETS - max_exact)).astype(jnp.int32)
    large = jnp.minimum(large, REL_BUCKETS - 1)
    return jnp.where(n < max_exact, n, large)


def _bias_lookup(dist, rb_rows):
    k = jnp.arange(REL_BUCKETS)
    return jnp.sum(jnp.where(t5_bucket(dist)[..., None] == k, rb_rows, 0.0), axis=-1)


def _select_blocks(gate, n_elig, nb):
    col = lax.broadcasted_iota(jnp.int32, gate.shape, 1)
    rank = jnp.zeros(gate.shape, jnp.int32)
    for m in range(nb):
        gm = gate[:, m:m + 1]
        beats = (gm > gate) | ((gm == gate) & (m < col))
        rank = rank + jnp.where(beats, jnp.where(m < n_elig, 1, 0), 0)
    return (col < n_elig) & (rank < MOBA_TOPK)


def _moba_prompt_kernel(q_ref, k_ref, v_ref, tb_ref, o_ref, kb_scr, vb_scr, *, nb, blk, scale):
    L = nb * blk
    kb_scr[...] = k_ref[0].astype(BF16)
    vb_scr[...] = v_ref[0].astype(BF16)
    kbar = jnp.concatenate(
        [jnp.mean(k_ref[0, j * blk:(j + 1) * blk, :], axis=0, keepdims=True) for j in range(nb)], axis=0)
    q = q_ref[0]
    gate = _fdot_nt(kbar, q)
    n_idx = lax.broadcasted_iota(jnp.int32, (nb, L), 0)
    own = lax.broadcasted_iota(jnp.int32, (nb, L), 1) // blk
    rank = jnp.zeros((nb, L), jnp.int32)
    for m in range(nb):
        gm = gate[m:m + 1, :]
        beats = (gm > gate) | ((gm == gate) & (m < n_idx))
        rank = rank + jnp.where(beats & (m < own), 1, 0)
    sel = jnp.where((n_idx < own) & (rank < MOBA_TOPK), 1.0, 0.0).T

    r = lax.broadcasted_iota(jnp.int32, (blk, blk), 0)
    c = lax.broadcasted_iota(jnp.int32, (blk, blk), 1)
    causal = c <= r
    far_bias = tb_ref[0, 0:1, 0:1]
    for i in range(nb):
        rows = slice(i * blk, (i + 1) * blk)
        s = _bdot_nt(q[rows], kb_scr[0:(i + 1) * blk, :]) * scale
        parts = []
        for j in range(i + 1):
            sj = s[:, j * blk:(j + 1) * blk]
            if j == i:
                parts.append(jnp.where(causal, sj + tb_ref[0, :, blk:], -jnp.inf))
            else:
                bias = tb_ref[0, :, :blk] if j == i - 1 else far_bias
                parts.append(jnp.where(sel[rows, j:j + 1] > 0.0, sj + bias, -jnp.inf))
        s = jnp.concatenate(parts, axis=1) if len(parts) > 1 else parts[0]
        m = jnp.max(s, axis=-1, keepdims=True)
        p = jnp.exp(s - m)
        l = jnp.sum(p, axis=-1, keepdims=True)
        o_ref[0, rows, :] = (_bdot(p, vb_scr[0:(i + 1) * blk, :]) / l).astype(o_ref.dtype)


def moba_prompt(proj, rel_bias, H):
    B, L, _ = proj.shape
    blk = MOBA_BLOCK
    dh = MOBA_HEAD_DIM
    nb = L // blk
    assert L % blk == 0 and blk > REL_MAX_DIST
    P = 2 * blk + 1
    mm = jnp.arange(P)
    dvec = jnp.where(mm <= blk, blk - mm, 2 * blk - 1)
    bvec = _bias_lookup(dvec[None, :], rel_bias.astype(F32).T[:, None, :])
    tb = jnp.tile(bvec, (1, blk))[:, :blk * (P - 1)].reshape(H, blk, P - 1)
    return pl.pallas_call(
        functools.partial(_moba_prompt_kernel, nb=nb, blk=blk, scale=dh ** -0.5),
        grid=(B, H),
        in_specs=[pl.BlockSpec((1, L, dh), lambda b, h: (b, 0, 4 * H + h)),
                  pl.BlockSpec((1, L, dh), lambda b, h: (b, 0, 5 * H + h)),
                  pl.BlockSpec((1, L, dh), lambda b, h: (b, 0, 6 * H + h)),
                  pl.BlockSpec((1, blk, 2 * blk), lambda b, h: (h, 0, 0))],
        out_specs=pl.BlockSpec((1, L, dh), lambda b, h: (b, 0, h)),
        out_shape=jax.ShapeDtypeStruct((B, L, H * dh), BF16),
        scratch_shapes=[pltpu.VMEM((L, dh), BF16), pltpu.VMEM((L, dh), BF16)],
        compiler_params=_params("parallel", "parallel"),
        name="moba_prompt",
    )(proj, proj, proj, tb)


def _moba_sample_kernel(pt_ref, q_ref, kn_ref, vn_ref, bown_ref, blast_ref, bfar_ref, *refs,
                        pps, ppb, ns, nbp, H, scale, inv_blk):
    k_refs = refs[:pps]
    v_refs = refs[pps:2 * pps]
    o_ref = refs[2 * pps]
    q_scr, p_scr, gate_scr, ksum_scr, l_scr, acc_scr = refs[2 * pps + 1:]
    s = pl.program_id(1)
    ds = q_ref.shape[1]
    dh = q_scr.shape[1]
    R = H * ds
    page = k_refs[0].shape[2] // H
    n_pages = ns * pps

    def head_major(ref):
        return jnp.concatenate([ref[0, :, h * dh:(h + 1) * dh] for h in range(H)], axis=0)

    @pl.when(s == 0)
    def _():
        q_scr[...] = head_major(q_ref)
        gate_scr[...] = jnp.zeros_like(gate_scr)

    @pl.when(s < ns)
    def _():
        lane = lax.broadcasted_iota(jnp.int32, (ds, LANES), 1)
        for i in range(pps):
            pg = s * pps + i
            for h in range(H):
                rows = slice(h * ds, (h + 1) * ds)
                kh = k_refs[i][0, 0, pl.ds(h, page, stride=H), :]
                p_scr[pg, rows, :] = _bdot_nt(q_scr[rows, :], kh) * scale
                tot = jnp.sum(kh, axis=0, keepdims=True)
                if i % ppb == 0:
                    ksum_scr[h:h + 1, :] = tot
                else:
                    ksum_scr[h:h + 1, :] += tot
                if i % ppb == ppb - 1:
                    g = jnp.sum(q_scr[rows, :] * (ksum_scr[h:h + 1, :] * inv_blk), axis=-1, keepdims=True)
                    gate_scr[rows, :] += jnp.where(lane == pg // ppb, g, 0.0)

    @pl.when(s == ns)
    def _():
        sel = jnp.where(_select_blocks(gate_scr[...], nbp, nbp), 1.0, 0.0)
        bcol = lax.broadcasted_iota(jnp.int32, (R, LANES), 1)
        far = bfar_ref[:, 0:1]
        q = q_scr[...]
        kn = head_major(kn_ref)
        vn = head_major(vn_ref)
        so = _bdot_nt(q, kn) * scale + bown_ref[...]
        r = lax.broadcasted_iota(jnp.int32, (R, R), 0)
        c = lax.broadcasted_iota(jnp.int32, (R, R), 1)
        so = jnp.where((c // ds == r // ds) & (c % ds <= r % ds), so, -jnp.inf)
        m_own = jnp.max(so, axis=-1, keepdims=True)

        def masked(pg):
            picked = jnp.sum(jnp.where(bcol == pg // ppb, sel, 0.0), axis=1, keepdims=True) > 0.0
            return jnp.where(picked, p_scr[pg], -jnp.inf)

        def max_body(pg, mx):
            return jnp.maximum(mx, masked(pg))

        mx = lax.fori_loop(0, n_pages - 1, max_body, jnp.full((R, LANES), -jnp.inf, F32))
        last = masked(n_pages - 1) + blast_ref[...]
        m = jnp.maximum(jnp.max(jnp.maximum(mx + far, last), axis=-1, keepdims=True), m_own)

        def exp_body(pg, acc):
            p = jnp.exp(masked(pg) + far - m)
            p_scr[pg] = p
            return acc + p

        lsum = lax.fori_loop(0, n_pages - 1, exp_body, jnp.zeros((R, LANES), F32))
        p_last = jnp.exp(last - m)
        p_scr[n_pages - 1] = p_last
        p_own = jnp.exp(so - m)
        l_scr[...] = (jnp.sum(lsum + p_last, axis=-1, keepdims=True) + jnp.sum(p_own, axis=-1, keepdims=True))
        acc_scr[...] = _bdot(p_own, vn)

    @pl.when(s >= ns)
    def _():
        for i in range(pps):
            pg = (s - ns) * pps + i
            for h in range(H):
                rows = slice(h * ds, (h + 1) * ds)
                vh = v_refs[i][0, 0, pl.ds(h, page, stride=H), :]
                acc_scr[rows, :] += _bdot(p_scr[pg, rows, :], vh)

        @pl.when(s == 2 * ns - 1)
        def _():
            o = acc_scr[...] / l_scr[...]
            for h in range(H):
                o_ref[0, :, h * dh:(h + 1) * dh] = o[h * ds:(h + 1) * ds, :].astype(o_ref.dtype)


def moba_sample(proj, cache_k, cache_v, layer, page_table, rel_bias):
    DB, DS, _ = proj.shape
    _, n_pool, page, H, dh = cache_k.shape
    n_pages = page_table.shape[1]
    past_len = n_pages * page
    ppb = MOBA_BLOCK // page
    pps = KV_PAGES_PER_STEP
    nbp = past_len // MOBA_BLOCK
    assert past_len % MOBA_BLOCK == 0 and DS <= MOBA_BLOCK and page >= REL_MAX_DIST and DS <= page
    assert MOBA_BLOCK % page == 0 and n_pages % pps == 0 and pps % ppb == 0 and nbp <= LANES and page == LANES
    ns = n_pages // pps
    R = H * DS
    BW = H * dh
    rb_rows = jnp.repeat(rel_bias.astype(F32).T, DS, axis=0)[:, None, :]
    hq_q = jnp.tile(jnp.arange(DS), H)
    b_own = _bias_lookup(jnp.maximum(hq_q[:, None] - hq_q[None, :], 0), rb_rows)
    b_last = _bias_lookup(page + hq_q[:, None] - jnp.arange(page)[None, :], rb_rows)
    b_far = jnp.broadcast_to(rb_rows[:, :, REL_BUCKETS - 1], (R, LANES))
    ck = cache_k.reshape(cache_k.shape[0], n_pool, page * H, dh)
    cv = cache_v.reshape(cache_v.shape[0], n_pool, page * H, dh)

    def kspec(i):
        return pl.BlockSpec((1, 1, page * H, dh),
                            lambda b, s, pt: (layer, pt[b, jnp.minimum(s, ns - 1) * pps + i], 0, 0))

    def vspec(i):
        return pl.BlockSpec((1, 1, page * H, dh),
                            lambda b, s, pt: (layer, pt[b, jnp.maximum(s - ns, 0) * pps + i], 0, 0))

    def slot(sl):
        return pl.BlockSpec((1, DS, BW), lambda b, s, pt: (b, 0, sl))

    const = lambda a: pl.BlockSpec(a.shape, lambda b, s, pt: (0,) * a.ndim)
    return pl.pallas_call(
        functools.partial(_moba_sample_kernel, pps=pps, ppb=ppb, ns=ns, nbp=nbp, H=H, scale=dh ** -0.5,
                          inv_blk=1.0 / MOBA_BLOCK),
        grid_spec=pltpu.PrefetchScalarGridSpec(
            num_scalar_prefetch=1,
            grid=(DB, 2 * ns),
            in_specs=[slot(4), slot(5), slot(6), const(b_own), const(b_last), const(b_far)]
                     + [kspec(i) for i in range(pps)] + [vspec(i) for i in range(pps)],
            out_specs=pl.BlockSpec((1, DS, BW), lambda b, s, pt: (b, 0, 0)),
            scratch_shapes=[pltpu.VMEM((R, dh), F32), pltpu.VMEM((n_pages, R, page), F32),
                            pltpu.VMEM((R, LANES), F32), pltpu.VMEM((H, dh), F32),
                            pltpu.VMEM((R, 1), F32), pltpu.VMEM((R, dh), F32)]),
        out_shape=jax.ShapeDtypeStruct((DB, DS, BW), BF16),
        compiler_params=_params("parallel", "arbitrary"),
        name="moba_sample",
    )(page_table, proj, proj, proj, b_own, b_last, b_far, *([ck] * pps), *([cv] * pps))


def _conv_kernel(p7, p8, p9, p10, p11, scb, cfb, sw, cw, cb, nw, nbias, oc_ref, od_ref, scn_ref, cfn_ref,
                 extc, extd, ybuf, *, tl, nt, ws, wd):
    t = pl.program_id(1)
    P = CONV_PAD
    W = ybuf.shape[1]

    @pl.when(t == 0)
    def _():
        extc[pl.ds(P - (ws - 1), ws - 1), :] = scb[0]
        extd[pl.ds(P - (wd - 1), wd - 1), :] = cfb[0]

    extc[pl.ds(P, tl), :] = p8[0] * p9[0]
    extd[pl.ds(P, tl), :] = p10[0] * jax.nn.sigmoid(p11[0])

    yc = sw[0:1, :] * extc[pl.ds(P - (ws - 1), tl), :]
    for j in range(1, ws):
        yc = yc + sw[j:j + 1, :] * extc[pl.ds(P - (ws - 1) + j, tl), :]
    oc_ref[0] = (p7[0] * yc).astype(oc_ref.dtype)

    for c0 in range(0, W, LANES):
        cols = slice(c0, c0 + LANES)
        acc = cw[0:1, cols] * extd[pl.ds(P - (wd - 1), tl), cols]
        for j in range(1, wd):
            acc = acc + cw[j:j + 1, cols] * extd[pl.ds(P - (wd - 1) + j, tl), cols]
        ybuf[:, cols] = acc + cb[:, cols]
    y = ybuf[...]
    yc_ = y - jnp.mean(y, axis=-1, keepdims=True)
    yn = yc_ * lax.rsqrt(jnp.mean(yc_ * yc_, axis=-1, keepdims=True) + EPS) * nw[...] + nbias[...]
    od_ref[0] = (yn * jax.nn.sigmoid(yn)).astype(od_ref.dtype)

    newc = extc[pl.ds(P + tl - (ws - 1), ws - 1), :]
    newd = extd[pl.ds(P + tl - (wd - 1), wd - 1), :]
    extc[pl.ds(P - (ws - 1), ws - 1), :] = newc
    extd[pl.ds(P - (wd - 1), wd - 1), :] = newd

    @pl.when(t == nt - 1)
    def _():
        scn_ref[0] = newc
        cfn_ref[0] = newd


def conv_branches(proj, sc_buf, cf_buf, sconv_w, cconv_w, cconv_b, cnorm_w, cnorm_b):
    B, L, _ = proj.shape
    ws, BW = sconv_w.shape
    wd = cconv_w.shape[0]
    assert wd - 1 <= CONV_PAD
    tl = _pick(L, (128, 64, 32, 16, 8))
    nt = L // tl

    def slot(s):
        return pl.BlockSpec((1, tl, BW), lambda b, t: (b, t, s))

    full = lambda a: pl.BlockSpec(a.shape, lambda b, t: (0,) * a.ndim)
    cb2, nw2, nb2 = cconv_b.reshape(1, BW), cnorm_w.reshape(1, BW), cnorm_b.reshape(1, BW)
    ospec = pl.BlockSpec((1, tl, BW), lambda b, t: (b, t, 0))
    return pl.pallas_call(
        functools.partial(_conv_kernel, tl=tl, nt=nt, ws=ws, wd=wd),
        grid=(B, nt),
        in_specs=[slot(7), slot(8), slot(9), slot(10), slot(11),
                  pl.BlockSpec((1, ws - 1, BW), lambda b, t: (b, 0, 0)),
                  pl.BlockSpec((1, wd - 1, BW), lambda b, t: (b, 0, 0)),
                  full(sconv_w), full(cconv_w), full(cb2), full(nw2), full(nb2)],
        out_specs=[ospec, ospec,
                   pl.BlockSpec((1, ws - 1, BW), lambda b, t: (b, 0, 0)),
                   pl.BlockSpec((1, wd - 1, BW), lambda b, t: (b, 0, 0))],
        out_shape=[jax.ShapeDtypeStruct((B, L, BW), BF16), jax.ShapeDtypeStruct((B, L, BW), BF16),
                   jax.ShapeDtypeStruct((B, ws - 1, BW), F32), jax.ShapeDtypeStruct((B, wd - 1, BW), F32)],
        scratch_shapes=[pltpu.VMEM((CONV_PAD + tl, BW), F32), pltpu.VMEM((CONV_PAD + tl, BW), F32),
                        pltpu.VMEM((tl, BW), F32)],
        compiler_params=_params("parallel", "arbitrary"),
        name="conv_branches",
    )(proj, proj, proj, proj, proj, sc_buf, cf_buf, sconv_w, cconv_w, cb2, nw2, nb2)


def _branches(proj, pos, past, layer, ret_state, sc_buf, cf_buf, lw):
    (ret_gn_w, sconv_w, cconv_w, cconv_b, cnorm_w, cnorm_b, rel_bias) = lw
    B, L, W = proj.shape
    BW = W // N_IN_SLOTS
    H = BW // MOBA_HEAD_DIM
    o_a, ret_new = retention(proj, pos, ret_state, ret_gn_w)
    if past is None:
        o_b = moba_prompt(proj, rel_bias, H)
    else:
        cache_k, cache_v, page_table = past
        o_b = moba_sample(proj, cache_k, cache_v, layer, page_table, rel_bias)
    o_c, o_d, sc_new, cf_new = conv_branches(proj, sc_buf, cf_buf, sconv_w, cconv_w, cconv_b, cnorm_w, cnorm_b)
    k_b = proj[:, :, 5 * BW:6 * BW].reshape(B, L, H, MOBA_HEAD_DIM)
    v_b = proj[:, :, 6 * BW:7 * BW].reshape(B, L, H, MOBA_HEAD_DIM)
    return [a.reshape(B * L, BW) for a in (o_a, o_b, o_c, o_d)], (k_b, v_b, ret_new, sc_new, cf_new)


def kernel(x_prompt, x_sample, c_prompt, c_sample, cache_k, cache_v, state_ret, state_sconv, state_cconv, page_table, w_ada, b_ada, norm_g, w_in, w_gate, b_gate, ret_gn_w, sconv_w, cconv_w, cconv_b, cnorm_w, cnorm_b, w_branch, w_o, w_ffn_in, w_ffn_out, rel_bias):
    B, S, D = x_prompt.shape
    DB, DS, _ = x_sample.shape
    depth = w_in.shape[0]
    BW = D // N_BRANCH
    H = BW // MOBA_HEAD_DIM
    dh_ret = BW // RET_HEADS
    F = w_ffn_out.shape[1]
    page = cache_k.shape[2]
    past_len = page_table.shape[1] * page
    pos_p = jnp.arange(S, dtype=jnp.int32)
    pos_s = past_len + jnp.arange(DS, dtype=jnp.int32)
    Mp, Ms = B * S, DB * DS

    rows = B + DB
    rows_pad = -(-rows // 8) * 8
    c_all = jnp.concatenate([c_prompt, c_sample, jnp.zeros((rows_pad - rows, D), F32)], axis=0)
    mods = ada_mod(c_all, w_ada, b_ada).reshape(depth, rows_pad * N_MOD, 1, D)
    w_ffn_out_b = w_ffn_out.astype(BF16)

    xp, xs = x_prompt, x_sample
    hp = norm_mod(xp, norm_g[0, 0], mods[0], 0, (1, 0))
    hs = norm_mod(xs, norm_g[0, 0], mods[0], B, (1, 0))
    zeros_ret = jnp.zeros((B, RET_HEADS, dh_ret, dh_ret), F32)
    zeros_sc = jnp.zeros((B, sconv_w.shape[1] - 1, BW), F32)
    zeros_cf = jnp.zeros((B, cconv_w.shape[1] - 1, BW), F32)
    outs_p, outs_s = [], []
    for l in range(depth):
        mod = mods[l]
        lw = (ret_gn_w[l], sconv_w[l], cconv_w[l], cconv_b[l], cnorm_w[l], cnorm_b[l], rel_bias)
        nxt = (norm_g[l + 1, 0], mods[l + 1], 1, 0) if l + 1 < depth else None
        hp2, hs2 = hp.reshape(Mp, D), hs.reshape(Ms, D)
        proj_p, proj_s = matmul(hp2, hs2, w_in, l, tm=2048, tn=256)
        br_p, op = _branches(proj_p.reshape(B, S, -1), pos_p, None, l, zeros_ret, zeros_sc, zeros_cf, lw)
        br_s, os_ = _branches(proj_s.reshape(DB, DS, -1), pos_s, (cache_k, cache_v, page_table), l,
                              state_ret[l], state_sconv[l], state_cconv[l], lw)
        mg_p, mg_s = gate_merge(hp2, hs2, br_p, br_s, w_gate, b_gate[l], w_branch, l)
        y_p, y_s = matmul(mg_p, mg_s, w_o, l, tm=2048, tn=256)
        ffn_mod = (norm_g[l, 2], mod, 4, 3)
        xp, hp = resid_norm(xp, y_p.reshape(B, S, D), norm_g[l, 1], mod, 0, 2, nxt=ffn_mod)
        xs, hs = resid_norm(xs, y_s.reshape(DB, DS, D), norm_g[l, 1], mod, B, 2, nxt=ffn_mod)
        act_p, act_s = ffn_in(hp.reshape(Mp, D), hs.reshape(Ms, D), w_ffn_in, l)
        y_p, y_s = matmul(act_p, act_s, w_ffn_out_b, l, tm=1024, tn=512, tk=F // 2)
        xp, hp = resid_norm(xp, y_p.reshape(B, S, D), norm_g[l, 3], mod, 0, 5, nxt=nxt)
        xs, hs = resid_norm(xs, y_s.reshape(DB, DS, D), norm_g[l, 3], mod, B, 5, nxt=nxt)
        outs_p.append(op)
        outs_s.append(os_)
    stack = lambda outs, i: jnp.stack([o[i] for o in outs])
    new_k_prompt = stack(outs_p, 0).reshape(depth, B, S // page, page, H, MOBA_HEAD_DIM)
    new_v_prompt = stack(outs_p, 1).reshape(depth, B, S // page, page, H, MOBA_HEAD_DIM)
    return (xp, xs, new_k_prompt, new_v_prompt, stack(outs_s, 0), stack(outs_s, 1),
            stack(outs_p, 2), stack(outs_s, 2), stack(outs_p, 3), stack(outs_s, 3),
            stack(outs_p, 4), stack(outs_s, 4))
```

```python
import functools
import math

import jax
import jax.numpy as jnp
from jax import lax
from jax.experimental import pallas as pl
from jax.experimental.pallas import tpu as pltpu

F32 = jnp.float32
BF16 = jnp.bfloat16

N_BRANCH = 4
N_IN_SLOTS = 12
RET_HEADS = 4
RET_CHUNK = 128
ROPE_BASE = 10000.0
MOBA_HEAD_DIM = 128
MOBA_BLOCK = 256
MOBA_TOPK = 3
REL_BUCKETS = 32
REL_MAX_DIST = 128
N_MOD = 6
EPS = 1e-6
GN_EPS = 1e-5
CONV_PAD = 32
LANES = 128
SUBLANES = 8
MXU_COLS = 256
KV_PAGES_PER_STEP = 8

VMEM_LIMIT_BYTES = 56 * 1024 * 1024


def _params(*sem):
    return pltpu.CompilerParams(dimension_semantics=sem, vmem_limit_bytes=VMEM_LIMIT_BYTES)


def _pick(dim, prefs):
    for p in prefs:
        if dim % p == 0:
            return p
    return dim


def _bdot(a, b):
    return jnp.dot(a.astype(BF16), b.astype(BF16), preferred_element_type=F32)


def _bdot_nt(a, b):
    return lax.dot_general(a.astype(BF16), b.astype(BF16), (((1,), (1,)), ((), ())), preferred_element_type=F32)


def _bdot_tn(a, b):
    return lax.dot_general(a.astype(BF16), b.astype(BF16), (((0,), (0,)), ((), ())), preferred_element_type=F32)


def _fdot_nt(a, b):
    return lax.dot_general(a, b, (((1,), (1,)), ((), ())), precision=lax.Precision.HIGHEST, preferred_element_type=F32)


def _resident(shape, imap):
    return pl.BlockSpec(shape, imap, pipeline_mode=pl.Buffered(1))


def _mm_kernel(xp_ref, xs_ref, w_ref, op_ref, os_ref, *, nk):
    i = pl.program_id(0)
    if nk == 1:
        for c0 in range(0, w_ref.shape[1], MXU_COLS):
            cols = slice(c0, c0 + MXU_COLS)
            w = w_ref[:, cols].astype(BF16)
            op_ref[:, cols] = jnp.dot(xp_ref[...], w, preferred_element_type=F32).astype(op_ref.dtype)

        @pl.when(i == 0)
        def _():
            w = w_ref[...].astype(BF16)
            os_ref[0] = jnp.dot(xs_ref[...], w, preferred_element_type=F32).astype(os_ref.dtype)
    else:
        w = w_ref[...].astype(BF16)
        part = jnp.dot(xp_ref[...], w, preferred_element_type=F32)
        k = pl.program_id(2)

        @pl.when(k == 0)
        def _():
            op_ref[...] = part

        @pl.when(k > 0)
        def _():
            op_ref[...] += part

        @pl.when((i == 0) & (k == 0))
        def _():
            os_ref[0] = jnp.dot(xs_ref[...], w, preferred_element_type=F32)

        @pl.when((i == 0) & (k > 0))
        def _():
            os_ref[0] += jnp.dot(xs_ref[...], w, preferred_element_type=F32)


def _unblock(o_s):
    nj, Ms, tn = o_s.shape
    return o_s.transpose(1, 0, 2).reshape(Ms, nj * tn)


def matmul(xp, xs, w, layer, *, tm, tn, tk=None, out_dtype=F32):
    Mp, K = xp.shape
    Ms = xs.shape[0]
    N = w.shape[2]
    tm = min(tm, Mp)
    tk = tk or K
    nk = K // tk
    nj = N // tn
    assert Mp % tm == 0 and N % tn == 0 and K % tk == 0 and tn % MXU_COLS == 0
    assert nk == 1 or out_dtype == F32
    xp_spec = (_resident((tm, tk), lambda i, j, k: (i, k)) if nk == 1
               else pl.BlockSpec((tm, tk), lambda i, j, k: (i, k)))
    o_p, o_s = pl.pallas_call(
        functools.partial(_mm_kernel, nk=nk),
        grid=(Mp // tm, nj, nk),
        in_specs=[xp_spec,
                  pl.BlockSpec((Ms, tk), lambda i, j, k: (0, k)),
                  pl.BlockSpec((None, tk, tn), lambda i, j, k: (layer, k, j))],
        out_specs=[pl.BlockSpec((tm, tn), lambda i, j, k: (i, j)),
                   pl.BlockSpec((1, Ms, tn), lambda i, j, k: (jnp.where(i == 0, j, nj - 1), 0, 0))],
        out_shape=[jax.ShapeDtypeStruct((Mp, N), out_dtype), jax.ShapeDtypeStruct((nj, Ms, tn), out_dtype)],
        compiler_params=_params("arbitrary", "arbitrary", "arbitrary"),
        name="matmul",
    )(xp, xs, w)
    return o_p, _unblock(o_s)


def _gate_merge_kernel(hp_ref, hs_ref, *refs):
    brp = refs[0:4]
    brs = refs[4:8]
    wg_ref, wb_ref, bg_ref, op_ref, os_ref, accp, accs = refs[8:]
    i = pl.program_id(0)
    n = pl.program_id(2)

    def term(h_ref, br_ref, cols):
        wg = wg_ref[:, cols].astype(BF16)
        wb = wb_ref[:, cols].astype(BF16)
        gate = jax.nn.sigmoid(jnp.dot(h_ref[...], wg, preferred_element_type=F32) + bg_ref[0][:, cols])
        return gate * jnp.dot(br_ref[...], wb, preferred_element_type=F32)

    for b in range(N_BRANCH):
        @pl.when(n == b)
        def _(b=b):
            for c0 in range(0, op_ref.shape[1], MXU_COLS):
                cols = slice(c0, c0 + MXU_COLS)
                t = term(hp_ref, brp[b], cols)
                if b == 0:
                    accp[:, cols] = t
                elif b < N_BRANCH - 1:
                    accp[:, cols] += t
                else:
                    op_ref[:, cols] = (accp[:, cols] + t).astype(op_ref.dtype)

        @pl.when((n == b) & (i == 0))
        def _(b=b):
            t = term(hs_ref, brs[b], slice(None))
            if b == 0:
                accs[...] = t
            elif b < N_BRANCH - 1:
                accs[...] += t
            else:
                os_ref[0] = (accs[...] + t).astype(os_ref.dtype)


def gate_merge(hp, hs, branches_p, branches_s, w_gate, b_gate, w_branch, layer, *, tm=1024, tn=512):
    Mp, D = hp.shape
    Ms = hs.shape[0]
    BW = branches_p[0].shape[1]
    tm = min(tm, Mp)
    nj = D // tn
    assert Mp % tm == 0 and D % tn == 0
    bg3 = b_gate.reshape(N_BRANCH, 1, D)
    in_specs = [_resident((tm, D), lambda i, j, n: (i, 0)),
                pl.BlockSpec((Ms, D), lambda i, j, n: (0, 0))]
    in_specs += [_resident((tm, BW), lambda i, j, n: (i, 0)) for _ in range(N_BRANCH)]
    in_specs += [pl.BlockSpec((Ms, BW), lambda i, j, n: (0, 0)) for _ in range(N_BRANCH)]
    in_specs += [pl.BlockSpec((None, D, tn), lambda i, j, n: (layer, 0, n * nj + j)),
                 pl.BlockSpec((None, None, BW, tn), lambda i, j, n: (layer, n, 0, j)),
                 pl.BlockSpec((1, 1, tn), lambda i, j, n: (n, 0, j))]
    o_p, o_s = pl.pallas_call(
        _gate_merge_kernel,
        grid=(Mp // tm, nj, N_BRANCH),
        in_specs=in_specs,
        out_specs=[pl.BlockSpec((tm, tn), lambda i, j, n: (i, j)),
                   pl.BlockSpec((1, Ms, tn), lambda i, j, n: (jnp.where(i == 0, j, nj - 1), 0, 0))],
        out_shape=[jax.ShapeDtypeStruct((Mp, D), BF16), jax.ShapeDtypeStruct((nj, Ms, tn), BF16)],
        scratch_shapes=[pltpu.VMEM((tm, tn), F32), pltpu.VMEM((Ms, tn), F32)],
        compiler_params=_params("arbitrary", "arbitrary", "arbitrary"),
        name="gate_merge",
    )(hp, hs, *branches_p, *branches_s, w_gate, w_branch, bg3)
    return o_p, _unblock(o_s)


def _ffn_in_kernel(hp_ref, hs_ref, wa_ref, wb_ref, op_ref, os_ref):
    wa = wa_ref[...].astype(BF16)
    wb = wb_ref[...].astype(BF16)

    def act(h_ref):
        a = jnp.dot(h_ref[...], wa, preferred_element_type=F32)
        b = jnp.dot(h_ref[...], wb, preferred_element_type=F32)
        return a * jax.nn.sigmoid(a) * b

    op_ref[...] = act(hp_ref).astype(op_ref.dtype)

    @pl.when(pl.program_id(0) == 0)
    def _():
        os_ref[0] = act(hs_ref).astype(os_ref.dtype)


def ffn_in(hp, hs, w, layer, *, tm=2048, tn=256):
    Mp, D = hp.shape
    Ms = hs.shape[0]
    F = w.shape[2] // 2
    tm = min(tm, Mp)
    nj = F // tn
    assert Mp % tm == 0 and F % tn == 0
    o_p, o_s = pl.pallas_call(
        _ffn_in_kernel,
        grid=(Mp // tm, nj),
        in_specs=[_resident((tm, D), lambda i, j: (i, 0)),
                  pl.BlockSpec((Ms, D), lambda i, j: (0, 0)),
                  pl.BlockSpec((None, D, tn), lambda i, j: (layer, 0, j)),
                  pl.BlockSpec((None, D, tn), lambda i, j: (layer, 0, nj + j))],
        out_specs=[pl.BlockSpec((tm, tn), lambda i, j: (i, j)),
                   pl.BlockSpec((1, Ms, tn), lambda i, j: (jnp.where(i == 0, j, nj - 1), 0, 0))],
        out_shape=[jax.ShapeDtypeStruct((Mp, F), BF16), jax.ShapeDtypeStruct((nj, Ms, tn), BF16)],
        compiler_params=_params("arbitrary", "arbitrary"),
        name="ffn_in",
    )(hp, hs, w, w)
    return o_p, _unblock(o_s)


def _ada_kernel(c_ref, w_ref, b_ref, o_ref):
    c = c_ref[...]
    x = c * jax.nn.sigmoid(c)
    o_ref[0] = _bdot(x, w_ref[0]) + b_ref[0]


def ada_mod(c_all, w_ada, b_ada):
    depth, D, N = w_ada.shape
    R = c_all.shape[0]
    tn = 512
    return pl.pallas_call(
        _ada_kernel,
        grid=(depth, N // tn),
        in_specs=[pl.BlockSpec((R, D), lambda l, j: (0, 0)),
                  pl.BlockSpec((1, D, tn), lambda l, j: (l, 0, j)),
                  pl.BlockSpec((1, 1, tn), lambda l, j: (l, 0, j))],
        out_specs=pl.BlockSpec((1, R, tn), lambda l, j: (l, 0, j)),
        out_shape=jax.ShapeDtypeStruct((depth, R, N), F32),
        compiler_params=_params("parallel", "parallel"),
        name="ada_mod",
    )(c_all, w_ada, b_ada.reshape(depth, 1, N))


def _rms(x, g):
    return x * lax.rsqrt(jnp.mean(x * x, axis=-1, keepdims=True) + EPS) * g


def _norm_mod_kernel(x_ref, g_ref, sc_ref, sh_ref, h_ref):
    h = _rms(x_ref[0], g_ref[...]) * (1.0 + sc_ref[0]) + sh_ref[0]
    h_ref[0] = h.astype(h_ref.dtype)


def _mod_spec(D, row0, slot):
    return pl.BlockSpec((1, 1, D), lambda b, t: ((row0 + b) * N_MOD + slot, 0, 0))


def norm_mod(x, g, mod, row0, slots):
    B, L, D = x.shape
    tl = _pick(L, (256, 128, 64, 32, 16, 8))
    return pl.pallas_call(
        _norm_mod_kernel,
        grid=(B, L // tl),
        in_specs=[pl.BlockSpec((1, tl, D), lambda b, t: (b, t, 0)),
                  pl.BlockSpec((1, D), lambda b, t: (0, 0)),
                  _mod_spec(D, row0, slots[0]), _mod_spec(D, row0, slots[1])],
        out_specs=pl.BlockSpec((1, tl, D), lambda b, t: (b, t, 0)),
        out_shape=jax.ShapeDtypeStruct((B, L, D), BF16),
        compiler_params=_params("parallel", "parallel"),
        name="norm_mod",
    )(x, g.reshape(1, D), mod, mod)


def _resid_kernel(x_ref, y_ref, ga_ref, gate_ref, *rest, emit_h):
    xn = x_ref[0] + gate_ref[0] * _rms(y_ref[0].astype(F32), ga_ref[...])
    if emit_h:
        gb_ref, sc_ref, sh_ref, xo_ref, h_ref = rest
        xo_ref[0] = xn
        h_ref[0] = (_rms(xn, gb_ref[...]) * (1.0 + sc_ref[0]) + sh_ref[0]).astype(h_ref.dtype)
    else:
        (xo_ref,) = rest
        xo_ref[0] = xn


def resid_norm(x, y, g_a, mod, row0, gate_slot, nxt=None):
    B, L, D = x.shape
    tl = _pick(L, (128, 64, 32, 16, 8))
    xspec = pl.BlockSpec((1, tl, D), lambda b, t: (b, t, 0))
    gspec = pl.BlockSpec((1, D), lambda b, t: (0, 0))
    in_specs = [xspec, xspec, gspec, _mod_spec(D, row0, gate_slot)]
    args = [x, y, g_a.reshape(1, D), mod]
    out_specs = [xspec]
    out_shape = [jax.ShapeDtypeStruct((B, L, D), F32)]
    if nxt is not None:
        g_b, mod_n, sc_slot, sh_slot = nxt
        in_specs += [gspec, _mod_spec(D, row0, sc_slot), _mod_spec(D, row0, sh_slot)]
        args += [g_b.reshape(1, D), mod_n, mod_n]
        out_specs.append(xspec)
        out_shape.append(jax.ShapeDtypeStruct((B, L, D), BF16))
    res = pl.pallas_call(
        functools.partial(_resid_kernel, emit_h=nxt is not None),
        grid=(B, L // tl),
        in_specs=in_specs, out_specs=out_specs, out_shape=out_shape,
        compiler_params=_params("parallel", "parallel"),
        name="resid_norm",
    )(*args)
    return (res[0], res[1]) if nxt is not None else (res[0], None)


def _ret_kernel(q_ref, k_ref, v_ref, g_ref, cos_ref, sin_ref, dec_ref, qd_ref, kd_ref, gl_ref, s0_ref, gnw_ref,
                o_ref, so_ref, s_scr, *, nc, dh):
    c = pl.program_id(1)

    @pl.when(c == 0)
    def _():
        s_scr[...] = s0_ref[0]

    half = dh // 2
    cos = cos_ref[...]
    sin = sin_ref[...]

    def rot(x):
        x1 = x[:, :half]
        x2 = x[:, half:]
        return jnp.concatenate([x1 * cos - x2 * sin, x1 * sin + x2 * cos], axis=-1)

    for h in range(s_scr.shape[0]):
        cols = slice(h * dh, (h + 1) * dh)
        q = rot(q_ref[0, :, cols])
        k = rot(k_ref[0, :, cols]) * (dh ** -0.5)
        v = v_ref[0, :, cols]
        S = s_scr[h]
        scores = _bdot_nt(q, k) * dec_ref[h]
        o = _bdot(scores, v) + _bdot(q * qd_ref[h], S)
        s_scr[h] = gl_ref[h, 0:1, :] * S + _bdot_tn(k * kd_ref[h], v)
        oc = o - jnp.mean(o, axis=-1, keepdims=True)
        on = oc * lax.rsqrt(jnp.mean(oc * oc, axis=-1, keepdims=True) + GN_EPS)
        g = g_ref[0, :, cols]
        o_ref[0, :, cols] = (on * gnw_ref[:, cols] * (g * jax.nn.sigmoid(g))).astype(o_ref.dtype)

    @pl.when(c == nc - 1)
    def _():
        so_ref[0] = s_scr[...]


def retention(proj, pos, state0, gn_w):
    B, L, _ = proj.shape
    H = RET_HEADS
    BW = gn_w.shape[0]
    dh = BW // H
    half = dh // 2
    C = RET_CHUNK if L % RET_CHUNK == 0 else L
    nc = L // C
    inv = jnp.exp(-math.log(ROPE_BASE) * jnp.arange(half, dtype=F32) / half)
    ang = pos.astype(F32)[:, None] * inv[None, :]
    cos, sin = jnp.cos(ang), jnp.sin(ang)
    log_g = jnp.log1p(-jnp.exp2(-5.0 - jnp.arange(H, dtype=F32)))
    idx = jnp.arange(C, dtype=F32)
    diff = idx[:, None] - idx[None, :]
    causal = diff >= 0
    dec = jnp.where(causal[None], jnp.exp(log_g[:, None, None] * jnp.where(causal, diff, 0.0)[None]), 0.0)
    qd = jnp.broadcast_to(jnp.exp(log_g[:, None] * (idx[None, :] + 1.0))[:, :, None], (H, C, dh))
    kd = jnp.broadcast_to(jnp.exp(log_g[:, None] * ((C - 1.0) - idx[None, :]))[:, :, None], (H, C, dh))
    gl = jnp.broadcast_to(jnp.exp(log_g * C)[:, None, None], (H, 8, dh))

    def slot(s):
        return pl.BlockSpec((1, C, BW), lambda b, c: (b, c, s))

    tab = pl.BlockSpec((C, half), lambda b, c: (c, 0))
    const = lambda a: pl.BlockSpec(a.shape, lambda b, c: (0,) * a.ndim)
    return pl.pallas_call(
        functools.partial(_ret_kernel, nc=nc, dh=dh),
        grid=(B, nc),
        in_specs=[slot(0), slot(1), slot(2), slot(3), tab, tab, const(dec), const(qd), const(kd), const(gl),
                  pl.BlockSpec((1, H, dh, dh), lambda b, c: (b, 0, 0, 0)),
                  pl.BlockSpec((1, BW), lambda b, c: (0, 0))],
        out_specs=[pl.BlockSpec((1, C, BW), lambda b, c: (b, c, 0)),
                   pl.BlockSpec((1, H, dh, dh), lambda b, c: (b, 0, 0, 0))],
        out_shape=[jax.ShapeDtypeStruct((B, L, BW), BF16),
                   jax.ShapeDtypeStruct((B, H, dh, dh), F32)],
        scratch_shapes=[pltpu.VMEM((H, dh, dh), F32)],
        compiler_params=_params("parallel", "arbitrary"),
        name="retention",
    )(proj, proj, proj, proj, cos, sin, dec, qd, kd, gl, state0, gn_w.reshape(1, BW))


def t5_bucket(n):
    max_exact = REL_BUCKETS // 2
    nf = jnp.maximum(n, 1).astype(F32)
    large = max_exact + (jnp.log(nf / max_exact) / math.log(REL_MAX_DIST / max_exact) * (REL_BUCKETS - max_exact)).astype(jnp.int32)
    large = jnp.minimum(large, REL_BUCKETS - 1)
    return jnp.where(n < max_exact, n, large)


def _bias_lookup(dist, rb_rows):
    k = jnp.arange(REL_BUCKETS)
    return jnp.sum(jnp.where(t5_bucket(dist)[..., None] == k, rb_rows, 0.0), axis=-1)


def _select_blocks(gate, n_elig, nb):
    col = lax.broadcasted_iota(jnp.int32, gate.shape, 1)
    rank = jnp.zeros(gate.shape, jnp.int32)
    for m in range(nb):
        gm = gate[:, m:m + 1]
        beats = (gm > gate) | ((gm == gate) & (m < col))
        rank = rank + jnp.where(beats, jnp.where(m < n_elig, 1, 0), 0)
    return (col < n_elig) & (rank < MOBA_TOPK)


def _moba_prompt_kernel(q_ref, k_ref, v_ref, tb_ref, o_ref, kb_scr, vb_scr, *, nb, blk, scale):
    L = nb * blk
    kb_scr[...] = k_ref[0].astype(BF16)
    vb_scr[...] = v_ref[0].astype(BF16)
    kbar = jnp.concatenate(
        [jnp.mean(k_ref[0, j * blk:(j + 1) * blk, :], axis=0, keepdims=True) for j in range(nb)], axis=0)
    q = q_ref[0]
    gate = _fdot_nt(kbar, q)
    n_idx = lax.broadcasted_iota(jnp.int32, (nb, L), 0)
    own = lax.broadcasted_iota(jnp.int32, (nb, L), 1) // blk
    rank = jnp.zeros((nb, L), jnp.int32)
    for m in range(nb):
        gm = gate[m:m + 1, :]
        beats = (gm > gate) | ((gm == gate) & (m < n_idx))
        rank = rank + jnp.where(beats & (m < own), 1, 0)
    sel = jnp.where((n_idx < own) & (rank < MOBA_TOPK), 1.0, 0.0).T

    r = lax.broadcasted_iota(jnp.int32, (blk, blk), 0)
    c = lax.broadcasted_iota(jnp.int32, (blk, blk), 1)
    causal = c <= r
    far_bias = tb_ref[0, 0:1, 0:1]
    for i in range(nb):
        rows = slice(i * blk, (i + 1) * blk)
        s = _bdot_nt(q[rows], kb_scr[0:(i + 1) * blk, :]) * scale
        parts = []
        for j in range(i + 1):
            sj = s[:, j * blk:(j + 1) * blk]
            if j == i:
                parts.append(jnp.where(causal, sj + tb_ref[0, :, blk:], -jnp.inf))
            else:
                bias = tb_ref[0, :, :blk] if j == i - 1 else far_bias
                parts.append(jnp.where(sel[rows, j:j + 1] > 0.0, sj + bias, -jnp.inf))
        s = jnp.concatenate(parts, axis=1) if len(parts) > 1 else parts[0]
        m = jnp.max(s, axis=-1, keepdims=True)
        p = jnp.exp(s - m)
        l = jnp.sum(p, axis=-1, keepdims=True)
        o_ref[0, rows, :] = (_bdot(p, vb_scr[0:(i + 1) * blk, :]) / l).astype(o_ref.dtype)


def moba_prompt(proj, rel_bias, H):
    B, L, _ = proj.shape
    blk = MOBA_BLOCK
    dh = MOBA_HEAD_DIM
    nb = L // blk
    assert L % blk == 0 and blk > REL_MAX_DIST
    P = 2 * blk + 1
    mm = jnp.arange(P)
    dvec = jnp.where(mm <= blk, blk - mm, 2 * blk - 1)
    bvec = _bias_lookup(dvec[None, :], rel_bias.astype(F32).T[:, None, :])
    tb = jnp.tile(bvec, (1, blk))[:, :blk * (P - 1)].reshape(H, blk, P - 1)
    return pl.pallas_call(
        functools.partial(_moba_prompt_kernel, nb=nb, blk=blk, scale=dh ** -0.5),
        grid=(B, H),
        in_specs=[pl.BlockSpec((1, L, dh), lambda b, h: (b, 0, 4 * H + h)),
                  pl.BlockSpec((1, L, dh), lambda b, h: (b, 0, 5 * H + h)),
                  pl.BlockSpec((1, L, dh), lambda b, h: (b, 0, 6 * H + h)),
                  pl.BlockSpec((1, blk, 2 * blk), lambda b, h: (h, 0, 0))],
        out_specs=pl.BlockSpec((1, L, dh), lambda b, h: (b, 0, h)),
        out_shape=jax.ShapeDtypeStruct((B, L, H * dh), BF16),
        scratch_shapes=[pltpu.VMEM((L, dh), BF16), pltpu.VMEM((L, dh), BF16)],
        compiler_params=_params("parallel", "parallel"),
        name="moba_prompt",
    )(proj, proj, proj, tb)


def _moba_sample_kernel(pt_ref, q_ref, kn_ref, vn_ref, bown_ref, blast_ref, bfar_ref, *refs,
                        pps, ppb, ns, nbp, H, scale, inv_blk):
    k_refs = refs[:pps]
    v_refs = refs[pps:2 * pps]
    o_ref = refs[2 * pps]
    q_scr, p_scr, gate_scr, ksum_scr, l_scr, acc_scr = refs[2 * pps + 1:]
    s = pl.program_id(1)
    ds = q_ref.shape[1]
    dh = q_scr.shape[1]
    R = H * ds
    page = k_refs[0].shape[2] // H
    n_pages = ns * pps

    def head_major(ref):
        return jnp.concatenate([ref[0, :, h * dh:(h + 1) * dh] for h in range(H)], axis=0)

    @pl.when(s == 0)
    def _():
        q_scr[...] = head_major(q_ref)
        gate_scr[...] = jnp.zeros_like(gate_scr)

    @pl.when(s < ns)
    def _():
        lane = lax.broadcasted_iota(jnp.int32, (ds, LANES), 1)
        for i in range(pps):
            pg = s * pps + i
            for h in range(H):
                rows = slice(h * ds, (h + 1) * ds)
                kh = k_refs[i][0, 0, pl.ds(h, page, stride=H), :]
                p_scr[pg, rows, :] = _bdot_nt(q_scr[rows, :], kh) * scale
                tot = jnp.sum(kh, axis=0, keepdims=True)
                if i % ppb == 0:
                    ksum_scr[h:h + 1, :] = tot
                else:
                    ksum_scr[h:h + 1, :] += tot
                if i % ppb == ppb - 1:
                    g = jnp.sum(q_scr[rows, :] * (ksum_scr[h:h + 1, :] * inv_blk), axis=-1, keepdims=True)
                    gate_scr[rows, :] += jnp.where(lane == pg // ppb, g, 0.0)

    @pl.when(s == ns)
    def _():
        sel = jnp.where(_select_blocks(gate_scr[...], nbp, nbp), 1.0, 0.0)
        bcol = lax.broadcasted_iota(jnp.int32, (R, LANES), 1)
        far = bfar_ref[:, 0:1]
        q = q_scr[...]
        kn = head_major(kn_ref)
        vn = head_major(vn_ref)
        so = _bdot_nt(q, kn) * scale + bown_ref[...]
        r = lax.broadcasted_iota(jnp.int32, (R, R), 0)
        c = lax.broadcasted_iota(jnp.int32, (R, R), 1)
        so = jnp.where((c // ds == r // ds) & (c % ds <= r % ds), so, -jnp.inf)
        m_own = jnp.max(so, axis=-1, keepdims=True)

        def masked(pg):
            picked = jnp.sum(jnp.where(bcol == pg // ppb, sel, 0.0), axis=1, keepdims=True) > 0.0
            return jnp.where(picked, p_scr[pg], -jnp.inf)

        def max_body(pg, mx):
            return jnp.maximum(mx, masked(pg))

        mx = lax.fori_loop(0, n_pages - 1, max_body, jnp.full((R, LANES), -jnp.inf, F32))
        last = masked(n_pages - 1) + blast_ref[...]
        m = jnp.maximum(jnp.max(jnp.maximum(mx + far, last), axis=-1, keepdims=True), m_own)

        def exp_body(pg, acc):
            p = jnp.exp(masked(pg) + far - m)
            p_scr[pg] = p
            return acc + p

        lsum = lax.fori_loop(0, n_pages - 1, exp_body, jnp.zeros((R, LANES), F32))
        p_last = jnp.exp(last - m)
        p_scr[n_pages - 1] = p_last
        p_own = jnp.exp(so - m)
        l_scr[...] = (jnp.sum(lsum + p_last, axis=-1, keepdims=True) + jnp.sum(p_own, axis=-1, keepdims=True))
        acc_scr[...] = _bdot(p_own, vn)

    @pl.when(s >= ns)
    def _():
        for i in range(pps):
            pg = (s - ns) * pps + i
            for h in range(H):
                rows = slice(h * ds, (h + 1) * ds)
                vh = v_refs[i][0, 0, pl.ds(h, page, stride=H), :]
                acc_scr[rows, :] += _bdot(p_scr[pg, rows, :], vh)

        @pl.when(s == 2 * ns - 1)
        def _():
            o = acc_scr[...] / l_scr[...]
            for h in range(H):
                o_ref[0, :, h * dh:(h + 1) * dh] = o[h * ds:(h + 1) * ds, :].astype(o_ref.dtype)


def moba_sample(proj, cache_k, cache_v, layer, page_table, rel_bias):
    DB, DS, _ = proj.shape
    _, n_pool, page, H, dh = cache_k.shape
    n_pages = page_table.shape[1]
    past_len = n_pages * page
    ppb = MOBA_BLOCK // page
    pps = KV_PAGES_PER_STEP
    nbp = past_len // MOBA_BLOCK
    assert past_len % MOBA_BLOCK == 0 and DS <= MOBA_BLOCK and page >= REL_MAX_DIST and DS <= page
    assert MOBA_BLOCK % page == 0 and n_pages % pps == 0 and pps % ppb == 0 and nbp <= LANES and page == LANES
    ns = n_pages // pps
    R = H * DS
    BW = H * dh
    rb_rows = jnp.repeat(rel_bias.astype(F32).T, DS, axis=0)[:, None, :]
    hq_q = jnp.tile(jnp.arange(DS), H)
    b_own = _bias_lookup(jnp.maximum(hq_q[:, None] - hq_q[None, :], 0), rb_rows)
    b_last = _bias_lookup(page + hq_q[:, None] - jnp.arange(page)[None, :], rb_rows)
    b_far = jnp.broadcast_to(rb_rows[:, :, REL_BUCKETS - 1], (R, LANES))
    ck = cache_k.reshape(cache_k.shape[0], n_pool, page * H, dh)
    cv = cache_v.reshape(cache_v.shape[0], n_pool, page * H, dh)

    def kspec(i):
        return pl.BlockSpec((1, 1, page * H, dh),
                            lambda b, s, pt: (layer, pt[b, jnp.minimum(s, ns - 1) * pps + i], 0, 0))

    def vspec(i):
        return pl.BlockSpec((1, 1, page * H, dh),
                            lambda b, s, pt: (layer, pt[b, jnp.maximum(s - ns, 0) * pps + i], 0, 0))

    def slot(sl):
        return pl.BlockSpec((1, DS, BW), lambda b, s, pt: (b, 0, sl))

    const = lambda a: pl.BlockSpec(a.shape, lambda b, s, pt: (0,) * a.ndim)
    return pl.pallas_call(
        functools.partial(_moba_sample_kernel, pps=pps, ppb=ppb, ns=ns, nbp=nbp, H=H, scale=dh ** -0.5,
                          inv_blk=1.0 / MOBA_BLOCK),
        grid_spec=pltpu.PrefetchScalarGridSpec(
            num_scalar_prefetch=1,
            grid=(DB, 2 * ns),
            in_specs=[slot(4), slot(5), slot(6), const(b_own), const(b_last), const(b_far)]
                     + [kspec(i) for i in range(pps)] + [vspec(i) for i in range(pps)],
            out_specs=pl.BlockSpec((1, DS, BW), lambda b, s, pt: (b, 0, 0)),
            scratch_shapes=[pltpu.VMEM((R, dh), F32), pltpu.VMEM((n_pages, R, page), F32),
                            pltpu.VMEM((R, LANES), F32), pltpu.VMEM((H, dh), F32),
                            pltpu.VMEM((R, 1), F32), pltpu.VMEM((R, dh), F32)]),
        out_shape=jax.ShapeDtypeStruct((DB, DS, BW), BF16),
        compiler_params=_params("parallel", "arbitrary"),
        name="moba_sample",
    )(page_table, proj, proj, proj, b_own, b_last, b_far, *([ck] * pps), *([cv] * pps))


def _conv_kernel(p7, p8, p9, p10, p11, scb, cfb, sw, cw, cb, nw, nbias, oc_ref, od_ref, scn_ref, cfn_ref,
                 extc, extd, ybuf, win, *, tl, nt, ws, wd):
    t = pl.program_id(1)
    P = CONV_PAD
    W = ybuf.shape[1]

    @pl.when(t == 0)
    def _():
        extc[pl.ds(P - (ws - 1), ws - 1), :] = scb[0]
        extd[pl.ds(P - (wd - 1), wd - 1), :] = cfb[0]

    extc[pl.ds(P, tl), :] = p8[0] * p9[0]
    extd[pl.ds(P, tl), :] = p10[0] * jax.nn.sigmoid(p11[0])

    yc = sw[0:1, :] * extc[pl.ds(P - (ws - 1), tl), :]
    for j in range(1, ws):
        yc = yc + sw[j:j + 1, :] * extc[pl.ds(P - (ws - 1) + j, tl), :]
    oc_ref[0] = (p7[0] * yc).astype(oc_ref.dtype)

    phases = min(SUBLANES, wd)
    for a in range(phases):
        rows = tl + (wd - 1 - a) // SUBLANES * SUBLANES
        win[a, pl.ds(0, rows), :] = extd[pl.ds(P - (wd - 1) + a, rows), :]
    for c0 in range(0, W, LANES):
        cols = slice(c0, c0 + LANES)
        acc = None
        for j in range(wd):
            a = j % phases
            term = cw[j:j + 1, cols] * win[a, pl.ds(j - a, tl), cols]
            acc = term if acc is None else acc + term
        ybuf[:, cols] = acc + cb[:, cols]
    y = ybuf[...]
    yc_ = y - jnp.mean(y, axis=-1, keepdims=True)
    yn = yc_ * lax.rsqrt(jnp.mean(yc_ * yc_, axis=-1, keepdims=True) + EPS) * nw[...] + nbias[...]
    od_ref[0] = (yn * jax.nn.sigmoid(yn)).astype(od_ref.dtype)

    newc = extc[pl.ds(P + tl - (ws - 1), ws - 1), :]
    newd = extd[pl.ds(P + tl - (wd - 1), wd - 1), :]
    extc[pl.ds(P - (ws - 1), ws - 1), :] = newc
    extd[pl.ds(P - (wd - 1), wd - 1), :] = newd

    @pl.when(t == nt - 1)
    def _():
        scn_ref[0] = newc
        cfn_ref[0] = newd


def conv_branches(proj, sc_buf, cf_buf, sconv_w, cconv_w, cconv_b, cnorm_w, cnorm_b):
    B, L, _ = proj.shape
    ws, BW = sconv_w.shape
    wd = cconv_w.shape[0]
    assert wd - 1 <= CONV_PAD
    tl = _pick(L, (128, 64, 32, 16, 8))
    nt = L // tl

    def slot(s):
        return pl.BlockSpec((1, tl, BW), lambda b, t: (b, t, s))

    full = lambda a: pl.BlockSpec(a.shape, lambda b, t: (0,) * a.ndim)
    cb2, nw2, nb2 = cconv_b.reshape(1, BW), cnorm_w.reshape(1, BW), cnorm_b.reshape(1, BW)
    ospec = pl.BlockSpec((1, tl, BW), lambda b, t: (b, t, 0))
    return pl.pallas_call(
        functools.partial(_conv_kernel, tl=tl, nt=nt, ws=ws, wd=wd),
        grid=(B, nt),
        in_specs=[slot(7), slot(8), slot(9), slot(10), slot(11),
                  pl.BlockSpec((1, ws - 1, BW), lambda b, t: (b, 0, 0)),
                  pl.BlockSpec((1, wd - 1, BW), lambda b, t: (b, 0, 0)),
                  full(sconv_w), full(cconv_w), full(cb2), full(nw2), full(nb2)],
        out_specs=[ospec, ospec,
                   pl.BlockSpec((1, ws - 1, BW), lambda b, t: (b, 0, 0)),
                   pl.BlockSpec((1, wd - 1, BW), lambda b, t: (b, 0, 0))],
        out_shape=[jax.ShapeDtypeStruct((B, L, BW), BF16), jax.ShapeDtypeStruct((B, L, BW), BF16),
                   jax.ShapeDtypeStruct((B, ws - 1, BW), F32), jax.ShapeDtypeStruct((B, wd - 1, BW), F32)],
        scratch_shapes=[pltpu.VMEM((CONV_PAD + tl, BW), F32), pltpu.VMEM((CONV_PAD + tl, BW), F32),
                        pltpu.VMEM((tl, BW), F32),
                        pltpu.VMEM((SUBLANES, tl + CONV_PAD - SUBLANES, BW), F32)],
        compiler_params=_params("parallel", "arbitrary"),
        name="conv_branches",
    )(proj, proj, proj, proj, proj, sc_buf, cf_buf, sconv_w, cconv_w, cb2, nw2, nb2)


def _branches(proj, pos, past, layer, ret_state, sc_buf, cf_buf, lw):
    (ret_gn_w, sconv_w, cconv_w, cconv_b, cnorm_w, cnorm_b, rel_bias) = lw
    B, L, W = proj.shape
    BW = W // N_IN_SLOTS
    H = BW // MOBA_HEAD_DIM
    o_a, ret_new = retention(proj, pos, ret_state, ret_gn_w)
    if past is None:
        o_b = moba_prompt(proj, rel_bias, H)
    else:
        cache_k, cache_v, page_table = past
        o_b = moba_sample(proj, cache_k, cache_v, layer, page_table, rel_bias)
    o_c, o_d, sc_new, cf_new = conv_branches(proj, sc_buf, cf_buf, sconv_w, cconv_w, cconv_b, cnorm_w, cnorm_b)
    k_b = proj[:, :, 5 * BW:6 * BW].reshape(B, L, H, MOBA_HEAD_DIM)
    v_b = proj[:, :, 6 * BW:7 * BW].reshape(B, L, H, MOBA_HEAD_DIM)
    return [a.reshape(B * L, BW) for a in (o_a, o_b, o_c, o_d)], (k_b, v_b, ret_new, sc_new, cf_new)


def kernel(x_prompt, x_sample, c_prompt, c_sample, cache_k, cache_v, state_ret, state_sconv, state_cconv, page_table, w_ada, b_ada, norm_g, w_in, w_gate, b_gate, ret_gn_w, sconv_w, cconv_w, cconv_b, cnorm_w, cnorm_b, w_branch, w_o, w_ffn_in, w_ffn_out, rel_bias):
    B, S, D = x_prompt.shape
    DB, DS, _ = x_sample.shape
    depth = w_in.shape[0]
    BW = D // N_BRANCH
    H = BW // MOBA_HEAD_DIM
    dh_ret = BW // RET_HEADS
    F = w_ffn_out.shape[1]
    page = cache_k.shape[2]
    past_len = page_table.shape[1] * page
    pos_p = jnp.arange(S, dtype=jnp.int32)
    pos_s = past_len + jnp.arange(DS, dtype=jnp.int32)
    Mp, Ms = B * S, DB * DS

    rows = B + DB
    rows_pad = -(-rows // 8) * 8
    c_all = jnp.concatenate([c_prompt, c_sample, jnp.zeros((rows_pad - rows, D), F32)], axis=0)
    mods = ada_mod(c_all, w_ada, b_ada).reshape(depth, rows_pad * N_MOD, 1, D)
    w_ffn_out_b = w_ffn_out.astype(BF16)

    xp, xs = x_prompt, x_sample
    hp = norm_mod(xp, norm_g[0, 0], mods[0], 0, (1, 0))
    hs = norm_mod(xs, norm_g[0, 0], mods[0], B, (1, 0))
    zeros_ret = jnp.zeros((B, RET_HEADS, dh_ret, dh_ret), F32)
    zeros_sc = jnp.zeros((B, sconv_w.shape[1] - 1, BW), F32)
    zeros_cf = jnp.zeros((B, cconv_w.shape[1] - 1, BW), F32)
    outs_p, outs_s = [], []
    for l in range(depth):
        mod = mods[l]
        lw = (ret_gn_w[l], sconv_w[l], cconv_w[l], cconv_b[l], cnorm_w[l], cnorm_b[l], rel_bias)
        nxt = (norm_g[l + 1, 0], mods[l + 1], 1, 0) if l + 1 < depth else None
        hp2, hs2 = hp.reshape(Mp, D), hs.reshape(Ms, D)
        proj_p, proj_s = matmul(hp2, hs2, w_in, l, tm=2048, tn=512)
        br_p, op = _branches(proj_p.reshape(B, S, -1), pos_p, None, l, zeros_ret, zeros_sc, zeros_cf, lw)
        br_s, os_ = _branches(proj_s.reshape(DB, DS, -1), pos_s, (cache_k, cache_v, page_table), l,
                              state_ret[l], state_sconv[l], state_cconv[l], lw)
        gm_tiles = dict(tm=1024, tn=512) if l == 0 else dict(tm=2048, tn=256)
        mg_p, mg_s = gate_merge(hp2, hs2, br_p, br_s, w_gate, b_gate[l], w_branch, l, **gm_tiles)
        y_p, y_s = matmul(mg_p, mg_s, w_o, l, tm=2048, tn=512, out_dtype=BF16)
        ffn_mod = (norm_g[l, 2], mod, 4, 3)
        xp, hp = resid_norm(xp, y_p.reshape(B, S, D), norm_g[l, 1], mod, 0, 2, nxt=ffn_mod)
        xs, hs = resid_norm(xs, y_s.reshape(DB, DS, D), norm_g[l, 1], mod, B, 2, nxt=ffn_mod)
        act_p, act_s = ffn_in(hp.reshape(Mp, D), hs.reshape(Ms, D), w_ffn_in, l)
        y_p, y_s = matmul(act_p, act_s, w_ffn_out_b, l, tm=1024, tn=512, tk=F // 2)
        xp, hp = resid_norm(xp, y_p.reshape(B, S, D), norm_g[l, 3], mod, 0, 5, nxt=nxt)
        xs, hs = resid_norm(xs, y_s.reshape(DB, DS, D), norm_g[l, 3], mod, B, 5, nxt=nxt)
        outs_p.append(op)
        outs_s.append(os_)
    stack = lambda outs, i: jnp.stack([o[i] for o in outs])
    new_k_prompt = stack(outs_p, 0).reshape(depth, B, S // page, page, H, MOBA_HEAD_DIM)
    new_v_prompt = stack(outs_p, 1).reshape(depth, B, S // page, page, H, MOBA_HEAD_DIM)
    return (xp, xs, new_k_prompt, new_v_prompt, stack(outs_s, 0), stack(outs_s, 1),
            stack(outs_p, 2), stack(outs_s, 2), stack(outs_p, 3), stack(outs_s, 3),
            stack(outs_p, 4), stack(outs_s, 4))
```

```python
import functools
import math

import jax
import jax.numpy as jnp
from jax import lax
from jax.experimental import pallas as pl
from jax.experimental.pallas import tpu as pltpu

F32 = jnp.float32
BF16 = jnp.bfloat16

N_BRANCH = 4
N_IN_SLOTS = 12
RET_HEADS = 4
RET_CHUNK = 128
ROPE_BASE = 10000.0
MOBA_HEAD_DIM = 128
MOBA_BLOCK = 256
MOBA_TOPK = 3
REL_BUCKETS = 32
REL_MAX_DIST = 128
N_MOD = 6
EPS = 1e-6
GN_EPS = 1e-5
CONV_PAD = 32
LANES = 128
SUBLANES = 8
MXU_COLS = 256
KV_PAGES_PER_STEP = 16

VMEM_LIMIT_BYTES = 56 * 1024 * 1024


def _params(*sem):
    return pltpu.CompilerParams(dimension_semantics=sem, vmem_limit_bytes=VMEM_LIMIT_BYTES)


def _pick(dim, prefs):
    for p in prefs:
        if dim % p == 0:
            return p
    return dim


def _bdot(a, b):
    return jnp.dot(a.astype(BF16), b.astype(BF16), preferred_element_type=F32)


def _bdot_nt(a, b):
    return lax.dot_general(a.astype(BF16), b.astype(BF16), (((1,), (1,)), ((), ())), preferred_element_type=F32)


def _bdot_tn(a, b):
    return lax.dot_general(a.astype(BF16), b.astype(BF16), (((0,), (0,)), ((), ())), preferred_element_type=F32)


def _fdot_nt(a, b):
    return lax.dot_general(a, b, (((1,), (1,)), ((), ())), precision=lax.Precision.HIGHEST, preferred_element_type=F32)


def _resident(shape, imap):
    return pl.BlockSpec(shape, imap, pipeline_mode=pl.Buffered(1))


def _mm_kernel(xp_ref, xs_ref, w_ref, op_ref, os_ref, *, nk):
    i = pl.program_id(0)
    if nk == 1:
        for c0 in range(0, w_ref.shape[1], MXU_COLS):
            cols = slice(c0, c0 + MXU_COLS)
            w = w_ref[:, cols].astype(BF16)
            op_ref[:, cols] = jnp.dot(xp_ref[...], w, preferred_element_type=F32).astype(op_ref.dtype)

        @pl.when(i == 0)
        def _():
            w = w_ref[...].astype(BF16)
            os_ref[0] = jnp.dot(xs_ref[...], w, preferred_element_type=F32).astype(os_ref.dtype)
    else:
        w = w_ref[...].astype(BF16)
        part = jnp.dot(xp_ref[...], w, preferred_element_type=F32)
        k = pl.program_id(2)

        @pl.when(k == 0)
        def _():
            op_ref[...] = part

        @pl.when(k > 0)
        def _():
            op_ref[...] += part

        @pl.when((i == 0) & (k == 0))
        def _():
            os_ref[0] = jnp.dot(xs_ref[...], w, preferred_element_type=F32)

        @pl.when((i == 0) & (k > 0))
        def _():
            os_ref[0] += jnp.dot(xs_ref[...], w, preferred_element_type=F32)


def _unblock(o_s):
    nj, Ms, tn = o_s.shape
    return o_s.transpose(1, 0, 2).reshape(Ms, nj * tn)


def matmul(xp, xs, w, layer, *, tm, tn, tk=None, out_dtype=F32):
    Mp, K = xp.shape
    Ms = xs.shape[0]
    N = w.shape[2]
    tm = min(tm, Mp)
    tk = tk or K
    nk = K // tk
    nj = N // tn
    assert Mp % tm == 0 and N % tn == 0 and K % tk == 0 and tn % MXU_COLS == 0
    assert nk == 1 or out_dtype == F32
    xp_spec = (_resident((tm, tk), lambda i, j, k: (i, k)) if nk == 1
               else pl.BlockSpec((tm, tk), lambda i, j, k: (i, k)))
    o_p, o_s = pl.pallas_call(
        functools.partial(_mm_kernel, nk=nk),
        grid=(Mp // tm, nj, nk),
        in_specs=[xp_spec,
                  pl.BlockSpec((Ms, tk), lambda i, j, k: (0, k)),
                  pl.BlockSpec((None, tk, tn), lambda i, j, k: (layer, k, j))],
        out_specs=[pl.BlockSpec((tm, tn), lambda i, j, k: (i, j)),
                   pl.BlockSpec((1, Ms, tn), lambda i, j, k: (jnp.where(i == 0, j, nj - 1), 0, 0))],
        out_shape=[jax.ShapeDtypeStruct((Mp, N), out_dtype), jax.ShapeDtypeStruct((nj, Ms, tn), out_dtype)],
        compiler_params=_params("arbitrary", "arbitrary", "arbitrary"),
        name="matmul",
    )(xp, xs, w)
    return o_p, _unblock(o_s)


def _gate_merge_kernel(hp_ref, hs_ref, *refs):
    brp = refs[0:4]
    brs = refs[4:8]
    wg_ref, wb_ref, bg_ref, op_ref, os_ref, accp, accs = refs[8:]
    i = pl.program_id(0)
    n = pl.program_id(2)

    def term(h_ref, br_ref, cols):
        wg = wg_ref[:, cols].astype(BF16)
        wb = wb_ref[:, cols].astype(BF16)
        gate = jax.nn.sigmoid(jnp.dot(h_ref[...], wg, preferred_element_type=F32) + bg_ref[0][:, cols])
        return gate * jnp.dot(br_ref[...], wb, preferred_element_type=F32)

    for b in range(N_BRANCH):
        @pl.when(n == b)
        def _(b=b):
            for c0 in range(0, op_ref.shape[1], MXU_COLS):
                cols = slice(c0, c0 + MXU_COLS)
                t = term(hp_ref, brp[b], cols)
                if b == 0:
                    accp[:, cols] = t
                elif b < N_BRANCH - 1:
                    accp[:, cols] += t
                else:
                    op_ref[:, cols] = (accp[:, cols] + t).astype(op_ref.dtype)

        @pl.when((n == b) & (i == 0))
        def _(b=b):
            t = term(hs_ref, brs[b], slice(None))
            if b == 0:
                accs[...] = t
            elif b < N_BRANCH - 1:
                accs[...] += t
            else:
                os_ref[0] = (accs[...] + t).astype(os_ref.dtype)


def gate_merge(hp, hs, branches_p, branches_s, w_gate, b_gate, w_branch, layer, *, tm=2048, tn=256):
    Mp, D = hp.shape
    Ms = hs.shape[0]
    BW = branches_p[0].shape[1]
    tm = min(tm, Mp)
    nj = D // tn
    assert Mp % tm == 0 and D % tn == 0
    bg3 = b_gate.reshape(N_BRANCH, 1, D)
    in_specs = [_resident((tm, D), lambda i, j, n: (i, 0)),
                pl.BlockSpec((Ms, D), lambda i, j, n: (0, 0))]
    in_specs += [_resident((tm, BW), lambda i, j, n: (i, 0)) for _ in range(N_BRANCH)]
    in_specs += [pl.BlockSpec((Ms, BW), lambda i, j, n: (0, 0)) for _ in range(N_BRANCH)]
    in_specs += [pl.BlockSpec((None, D, tn), lambda i, j, n: (layer, 0, n * nj + j)),
                 pl.BlockSpec((None, None, BW, tn), lambda i, j, n: (layer, n, 0, j)),
                 pl.BlockSpec((1, 1, tn), lambda i, j, n: (n, 0, j))]
    o_p, o_s = pl.pallas_call(
        _gate_merge_kernel,
        grid=(Mp // tm, nj, N_BRANCH),
        in_specs=in_specs,
        out_specs=[pl.BlockSpec((tm, tn), lambda i, j, n: (i, j)),
                   pl.BlockSpec((1, Ms, tn), lambda i, j, n: (jnp.where(i == 0, j, nj - 1), 0, 0))],
        out_shape=[jax.ShapeDtypeStruct((Mp, D), BF16), jax.ShapeDtypeStruct((nj, Ms, tn), BF16)],
        scratch_shapes=[pltpu.VMEM((tm, tn), F32), pltpu.VMEM((Ms, tn), F32)],
        compiler_params=_params("arbitrary", "arbitrary", "arbitrary"),
        name="gate_merge",
    )(hp, hs, *branches_p, *branches_s, w_gate, w_branch, bg3)
    return o_p, _unblock(o_s)


def _ffn_in_kernel(hp_ref, hs_ref, wa_ref, wb_ref, op_ref, os_ref):
    wa = wa_ref[...].astype(BF16)
    wb = wb_ref[...].astype(BF16)

    def act(h_ref):
        a = jnp.dot(h_ref[...], wa, preferred_element_type=F32)
        b = jnp.dot(h_ref[...], wb, preferred_element_type=F32)
        return a * jax.nn.sigmoid(a) * b

    op_ref[...] = act(hp_ref).astype(op_ref.dtype)

    @pl.when(pl.program_id(0) == 0)
    def _():
        os_ref[0] = act(hs_ref).astype(os_ref.dtype)


def ffn_in(hp, hs, w, layer, *, tm=2048, tn=256):
    Mp, D = hp.shape
    Ms = hs.shape[0]
    F = w.shape[2] // 2
    tm = min(tm, Mp)
    nj = F // tn
    assert Mp % tm == 0 and F % tn == 0
    o_p, o_s = pl.pallas_call(
        _ffn_in_kernel,
        grid=(Mp // tm, nj),
        in_specs=[_resident((tm, D), lambda i, j: (i, 0)),
                  pl.BlockSpec((Ms, D), lambda i, j: (0, 0)),
                  pl.BlockSpec((None, D, tn), lambda i, j: (layer, 0, j)),
                  pl.BlockSpec((None, D, tn), lambda i, j: (layer, 0, nj + j))],
        out_specs=[pl.BlockSpec((tm, tn), lambda i, j: (i, j)),
                   pl.BlockSpec((1, Ms, tn), lambda i, j: (jnp.where(i == 0, j, nj - 1), 0, 0))],
        out_shape=[jax.ShapeDtypeStruct((Mp, F), BF16), jax.ShapeDtypeStruct((nj, Ms, tn), BF16)],
        compiler_params=_params("arbitrary", "arbitrary"),
        name="ffn_in",
    )(hp, hs, w, w)
    return o_p, _unblock(o_s)


def _ada_kernel(c_ref, w_ref, b_ref, o_ref):
    c = c_ref[...]
    x = c * jax.nn.sigmoid(c)
    o_ref[0] = _bdot(x, w_ref[0]) + b_ref[0]


def ada_mod(c_all, w_ada, b_ada):
    depth, D, N = w_ada.shape
    R = c_all.shape[0]
    tn = 512
    return pl.pallas_call(
        _ada_kernel,
        grid=(depth, N // tn),
        in_specs=[pl.BlockSpec((R, D), lambda l, j: (0, 0)),
                  pl.BlockSpec((1, D, tn), lambda l, j: (l, 0, j)),
                  pl.BlockSpec((1, 1, tn), lambda l, j: (l, 0, j))],
        out_specs=pl.BlockSpec((1, R, tn), lambda l, j: (l, 0, j)),
        out_shape=jax.ShapeDtypeStruct((depth, R, N), F32),
        compiler_params=_params("parallel", "parallel"),
        name="ada_mod",
    )(c_all, w_ada, b_ada.reshape(depth, 1, N))


def _rms(x, g):
    return x * lax.rsqrt(jnp.mean(x * x, axis=-1, keepdims=True) + EPS) * g


def _norm_mod_kernel(x_ref, g_ref, sc_ref, sh_ref, h_ref):
    h = _rms(x_ref[0], g_ref[...]) * (1.0 + sc_ref[0]) + sh_ref[0]
    h_ref[0] = h.astype(h_ref.dtype)


def _mod_spec(D, row0, slot):
    return pl.BlockSpec((1, 1, D), lambda b, t: ((row0 + b) * N_MOD + slot, 0, 0))


def norm_mod(x, g, mod, row0, slots):
    B, L, D = x.shape
    tl = _pick(L, (256, 128, 64, 32, 16, 8))
    return pl.pallas_call(
        _norm_mod_kernel,
        grid=(B, L // tl),
        in_specs=[pl.BlockSpec((1, tl, D), lambda b, t: (b, t, 0)),
                  pl.BlockSpec((1, D), lambda b, t: (0, 0)),
                  _mod_spec(D, row0, slots[0]), _mod_spec(D, row0, slots[1])],
        out_specs=pl.BlockSpec((1, tl, D), lambda b, t: (b, t, 0)),
        out_shape=jax.ShapeDtypeStruct((B, L, D), BF16),
        compiler_params=_params("parallel", "parallel"),
        name="norm_mod",
    )(x, g.reshape(1, D), mod, mod)


def _resid_kernel(x_ref, y_ref, ga_ref, gate_ref, *rest, emit_h):
    xn = x_ref[0] + gate_ref[0] * _rms(y_ref[0].astype(F32), ga_ref[...])
    if emit_h:
        gb_ref, sc_ref, sh_ref, xo_ref, h_ref = rest
        xo_ref[0] = xn
        h_ref[0] = (_rms(xn, gb_ref[...]) * (1.0 + sc_ref[0]) + sh_ref[0]).astype(h_ref.dtype)
    else:
        (xo_ref,) = rest
        xo_ref[0] = xn


def resid_norm(x, y, g_a, mod, row0, gate_slot, nxt=None):
    B, L, D = x.shape
    tl = _pick(L, (128, 64, 32, 16, 8))
    xspec = pl.BlockSpec((1, tl, D), lambda b, t: (b, t, 0))
    gspec = pl.BlockSpec((1, D), lambda b, t: (0, 0))
    in_specs = [xspec, xspec, gspec, _mod_spec(D, row0, gate_slot)]
    args = [x, y, g_a.reshape(1, D), mod]
    out_specs = [xspec]
    out_shape = [jax.ShapeDtypeStruct((B, L, D), F32)]
    if nxt is not None:
        g_b, mod_n, sc_slot, sh_slot = nxt
        in_specs += [gspec, _mod_spec(D, row0, sc_slot), _mod_spec(D, row0, sh_slot)]
        args += [g_b.reshape(1, D), mod_n, mod_n]
        out_specs.append(xspec)
        out_shape.append(jax.ShapeDtypeStruct((B, L, D), BF16))
    res = pl.pallas_call(
        functools.partial(_resid_kernel, emit_h=nxt is not None),
        grid=(B, L // tl),
        in_specs=in_specs, out_specs=out_specs, out_shape=out_shape,
        compiler_params=_params("parallel", "parallel"),
        name="resid_norm",
    )(*args)
    return (res[0], res[1]) if nxt is not None else (res[0], None)


def _ret_kernel(q_ref, k_ref, v_ref, g_ref, cos_ref, sin_ref, dec_ref, qd_ref, kd_ref, gl_ref, s0_ref, gnw_ref,
                o_ref, so_ref, s_scr, *, nc, dh):
    c = pl.program_id(1)

    @pl.when(c == 0)
    def _():
        s_scr[...] = s0_ref[0]

    half = dh // 2
    cos = cos_ref[...]
    sin = sin_ref[...]

    def rot(x):
        x1 = x[:, :half]
        x2 = x[:, half:]
        return jnp.concatenate([x1 * cos - x2 * sin, x1 * sin + x2 * cos], axis=-1)

    for h in range(s_scr.shape[0]):
        cols = slice(h * dh, (h + 1) * dh)
        q = rot(q_ref[0, :, cols])
        k = rot(k_ref[0, :, cols]) * (dh ** -0.5)
        v = v_ref[0, :, cols]
        S = s_scr[h]
        scores = _bdot_nt(q, k) * dec_ref[h]
        o = _bdot(scores, v) + _bdot(q * qd_ref[h], S)
        s_scr[h] = gl_ref[h, 0:1, :] * S + _bdot_tn(k * kd_ref[h], v)
        oc = o - jnp.mean(o, axis=-1, keepdims=True)
        on = oc * lax.rsqrt(jnp.mean(oc * oc, axis=-1, keepdims=True) + GN_EPS)
        g = g_ref[0, :, cols]
        o_ref[0, :, cols] = (on * gnw_ref[:, cols] * (g * jax.nn.sigmoid(g))).astype(o_ref.dtype)

    @pl.when(c == nc - 1)
    def _():
        so_ref[0] = s_scr[...]


def retention(proj, pos, state0, gn_w):
    B, L, _ = proj.shape
    H = RET_HEADS
    BW = gn_w.shape[0]
    dh = BW // H
    half = dh // 2
    C = RET_CHUNK if L % RET_CHUNK == 0 else L
    nc = L // C
    inv = jnp.exp(-math.log(ROPE_BASE) * jnp.arange(half, dtype=F32) / half)
    ang = pos.astype(F32)[:, None] * inv[None, :]
    cos, sin = jnp.cos(ang), jnp.sin(ang)
    log_g = jnp.log1p(-jnp.exp2(-5.0 - jnp.arange(H, dtype=F32)))
    idx = jnp.arange(C, dtype=F32)
    diff = idx[:, None] - idx[None, :]
    causal = diff >= 0
    dec = jnp.where(causal[None], jnp.exp(log_g[:, None, None] * jnp.where(causal, diff, 0.0)[None]), 0.0)
    qd = jnp.broadcast_to(jnp.exp(log_g[:, None] * (idx[None, :] + 1.0))[:, :, None], (H, C, dh))
    kd = jnp.broadcast_to(jnp.exp(log_g[:, None] * ((C - 1.0) - idx[None, :]))[:, :, None], (H, C, dh))
    gl = jnp.broadcast_to(jnp.exp(log_g * C)[:, None, None], (H, 8, dh))

    def slot(s):
        return pl.BlockSpec((1, C, BW), lambda b, c: (b, c, s))

    tab = pl.BlockSpec((C, half), lambda b, c: (c, 0))
    const = lambda a: pl.BlockSpec(a.shape, lambda b, c: (0,) * a.ndim)
    return pl.pallas_call(
        functools.partial(_ret_kernel, nc=nc, dh=dh),
        grid=(B, nc),
        in_specs=[slot(0), slot(1), slot(2), slot(3), tab, tab, const(dec), const(qd), const(kd), const(gl),
                  pl.BlockSpec((1, H, dh, dh), lambda b, c: (b, 0, 0, 0)),
                  pl.BlockSpec((1, BW), lambda b, c: (0, 0))],
        out_specs=[pl.BlockSpec((1, C, BW), lambda b, c: (b, c, 0)),
                   pl.BlockSpec((1, H, dh, dh), lambda b, c: (b, 0, 0, 0))],
        out_shape=[jax.ShapeDtypeStruct((B, L, BW), BF16),
                   jax.ShapeDtypeStruct((B, H, dh, dh), F32)],
        scratch_shapes=[pltpu.VMEM((H, dh, dh), F32)],
        compiler_params=_params("parallel", "arbitrary"),
        name="retention",
    )(proj, proj, proj, proj, cos, sin, dec, qd, kd, gl, state0, gn_w.reshape(1, BW))


def t5_bucket(n):
    max_exact = REL_BUCKETS // 2
    nf = jnp.maximum(n, 1).astype(F32)
    large = max_exact + (jnp.log(nf / max_exact) / math.log(REL_MAX_DIST / max_exact) * (REL_BUCKETS - max_exact)).astype(jnp.int32)
    large = jnp.minimum(large, REL_BUCKETS - 1)
    return jnp.where(n < max_exact, n, large)


def _bias_lookup(dist, rb_rows):
    k = jnp.arange(REL_BUCKETS)
    return jnp.sum(jnp.where(t5_bucket(dist)[..., None] == k, rb_rows, 0.0), axis=-1)


def _select_blocks(gate, n_elig, nb):
    col = lax.broadcasted_iota(jnp.int32, gate.shape, 1)
    rank = jnp.zeros(gate.shape, jnp.int32)
    for m in range(nb):
        gm = gate[:, m:m + 1]
        beats = (gm > gate) | ((gm == gate) & (m < col))
        rank = rank + jnp.where(beats, jnp.where(m < n_elig, 1, 0), 0)
    return (col < n_elig) & (rank < MOBA_TOPK)


def _moba_prompt_kernel(q_ref, k_ref, v_ref, tb_ref, o_ref, kb_scr, vb_scr, *, nb, blk, scale):
    L = nb * blk
    kb_scr[...] = k_ref[0].astype(BF16)
    vb_scr[...] = v_ref[0].astype(BF16)
    kbar = jnp.concatenate(
        [jnp.mean(k_ref[0, j * blk:(j + 1) * blk, :], axis=0, keepdims=True) for j in range(nb)], axis=0)
    q = q_ref[0]
    gate = _fdot_nt(kbar, q)
    n_idx = lax.broadcasted_iota(jnp.int32, (nb, L), 0)
    own = lax.broadcasted_iota(jnp.int32, (nb, L), 1) // blk
    rank = jnp.zeros((nb, L), jnp.int32)
    for m in range(nb):
        gm = gate[m:m + 1, :]
        beats = (gm > gate) | ((gm == gate) & (m < n_idx))
        rank = rank + jnp.where(beats & (m < own), 1, 0)
    sel = jnp.where((n_idx < own) & (rank < MOBA_TOPK), 1.0, 0.0).T

    r = lax.broadcasted_iota(jnp.int32, (blk, blk), 0)
    c = lax.broadcasted_iota(jnp.int32, (blk, blk), 1)
    causal = c <= r
    far_bias = tb_ref[0, 0:1, 0:1]
    for i in range(nb):
        rows = slice(i * blk, (i + 1) * blk)
        s = _bdot_nt(q[rows], kb_scr[0:(i + 1) * blk, :]) * scale
        parts = []
        for j in range(i + 1):
            sj = s[:, j * blk:(j + 1) * blk]
            if j == i:
                parts.append(jnp.where(causal, sj + tb_ref[0, :, blk:], -jnp.inf))
            else:
                bias = tb_ref[0, :, :blk] if j == i - 1 else far_bias
                parts.append(jnp.where(sel[rows, j:j + 1] > 0.0, sj + bias, -jnp.inf))
        s = jnp.concatenate(parts, axis=1) if len(parts) > 1 else parts[0]
        m = jnp.max(s, axis=-1, keepdims=True)
        p = jnp.exp(s - m)
        l = jnp.sum(p, axis=-1, keepdims=True)
        o_ref[0, rows, :] = (_bdot(p, vb_scr[0:(i + 1) * blk, :]) / l).astype(o_ref.dtype)


def moba_prompt(proj, rel_bias, H):
    B, L, _ = proj.shape
    blk = MOBA_BLOCK
    dh = MOBA_HEAD_DIM
    nb = L // blk
    assert L % blk == 0 and blk > REL_MAX_DIST
    P = 2 * blk + 1
    mm = jnp.arange(P)
    dvec = jnp.where(mm <= blk, blk - mm, 2 * blk - 1)
    bvec = _bias_lookup(dvec[None, :], rel_bias.astype(F32).T[:, None, :])
    tb = jnp.tile(bvec, (1, blk))[:, :blk * (P - 1)].reshape(H, blk, P - 1)
    return pl.pallas_call(
        functools.partial(_moba_prompt_kernel, nb=nb, blk=blk, scale=dh ** -0.5),
        grid=(B, H),
        in_specs=[pl.BlockSpec((1, L, dh), lambda b, h: (b, 0, 4 * H + h)),
                  pl.BlockSpec((1, L, dh), lambda b, h: (b, 0, 5 * H + h)),
                  pl.BlockSpec((1, L, dh), lambda b, h: (b, 0, 6 * H + h)),
                  pl.BlockSpec((1, blk, 2 * blk), lambda b, h: (h, 0, 0))],
        out_specs=pl.BlockSpec((1, L, dh), lambda b, h: (b, 0, h)),
        out_shape=jax.ShapeDtypeStruct((B, L, H * dh), BF16),
        scratch_shapes=[pltpu.VMEM((L, dh), BF16), pltpu.VMEM((L, dh), BF16)],
        compiler_params=_params("parallel", "parallel"),
        name="moba_prompt",
    )(proj, proj, proj, tb)


def _moba_sample_kernel(pt_ref, q_ref, kn_ref, vn_ref, bown_ref, blast_ref, bfar_ref, *refs,
                        pps, ppb, ns, nbp, H, scale, inv_blk):
    k_refs = refs[:pps]
    v_refs = refs[pps:2 * pps]
    o_ref = refs[2 * pps]
    q_scr, p_scr, gate_scr, ksum_scr, l_scr, acc_scr = refs[2 * pps + 1:]
    s = pl.program_id(1)
    ds = q_ref.shape[1]
    dh = q_scr.shape[1]
    R = H * ds
    page = k_refs[0].shape[2] // H
    n_pages = ns * pps

    def head_major(ref):
        return jnp.concatenate([ref[0, :, h * dh:(h + 1) * dh] for h in range(H)], axis=0)

    @pl.when(s == 0)
    def _():
        q_scr[...] = head_major(q_ref)
        gate_scr[...] = jnp.zeros_like(gate_scr)

    @pl.when(s < ns)
    def _():
        lane = lax.broadcasted_iota(jnp.int32, (ds, LANES), 1)
        for i in range(pps):
            pg = s * pps + i
            for h in range(H):
                rows = slice(h * ds, (h + 1) * ds)
                kh = k_refs[i][0, 0, pl.ds(h, page, stride=H), :]
                p_scr[pg, rows, :] = _bdot_nt(q_scr[rows, :], kh) * scale
                tot = jnp.sum(kh, axis=0, keepdims=True)
                if i % ppb == 0:
                    ksum_scr[h:h + 1, :] = tot
                else:
                    ksum_scr[h:h + 1, :] += tot
                if i % ppb == ppb - 1:
                    g = jnp.sum(q_scr[rows, :] * (ksum_scr[h:h + 1, :] * inv_blk), axis=-1, keepdims=True)
                    gate_scr[rows, :] += jnp.where(lane == pg // ppb, g, 0.0)

    @pl.when(s == ns)
    def _():
        sel = jnp.where(_select_blocks(gate_scr[...], nbp, nbp), 1.0, 0.0)
        far = bfar_ref[:, 0:1]
        q = q_scr[...]
        kn = head_major(kn_ref)
        vn = head_major(vn_ref)
        so = _bdot_nt(q, kn) * scale + bown_ref[...]
        r = lax.broadcasted_iota(jnp.int32, (R, R), 0)
        c = lax.broadcasted_iota(jnp.int32, (R, R), 1)
        so = jnp.where((c // ds == r // ds) & (c % ds <= r % ds), so, -jnp.inf)
        m_own = jnp.max(so, axis=-1, keepdims=True)

        picked = [jnp.broadcast_to(sel[:, n:n + 1] > 0.0, (R, page)) for n in range(nbp)]

        def masked(pg):
            return jnp.where(picked[pg // ppb], p_scr[pg], -jnp.inf)

        mx = masked(0)
        for pg in range(1, n_pages - 1):
            mx = jnp.maximum(mx, masked(pg))
        last = masked(n_pages - 1) + blast_ref[...]
        m = jnp.maximum(jnp.max(jnp.maximum(mx + far, last), axis=-1, keepdims=True), m_own)
        shift = far - m
        lsum = jnp.exp(last - m)
        p_scr[n_pages - 1] = lsum
        for pg in range(n_pages - 1):
            p = jnp.exp(masked(pg) + shift)
            p_scr[pg] = p
            lsum = lsum + p
        p_own = jnp.exp(so - m)
        l_scr[...] = jnp.sum(lsum, axis=-1, keepdims=True) + jnp.sum(p_own, axis=-1, keepdims=True)
        acc_scr[...] = _bdot(p_own, vn)

    @pl.when(s >= ns)
    def _():
        for i in range(pps):
            pg = (s - ns) * pps + i
            for h in range(H):
                rows = slice(h * ds, (h + 1) * ds)
                vh = v_refs[i][0, 0, pl.ds(h, page, stride=H), :]
                acc_scr[rows, :] += _bdot(p_scr[pg, rows, :], vh)

        @pl.when(s == 2 * ns - 1)
        def _():
            o = acc_scr[...] / l_scr[...]
            for h in range(H):
                o_ref[0, :, h * dh:(h + 1) * dh] = o[h * ds:(h + 1) * ds, :].astype(o_ref.dtype)


def moba_sample(proj, cache_k, cache_v, layer, page_table, rel_bias):
    DB, DS, _ = proj.shape
    _, n_pool, page, H, dh = cache_k.shape
    n_pages = page_table.shape[1]
    past_len = n_pages * page
    ppb = MOBA_BLOCK // page
    pps = min(KV_PAGES_PER_STEP, n_pages)
    nbp = past_len // MOBA_BLOCK
    assert past_len % MOBA_BLOCK == 0 and DS <= MOBA_BLOCK and page >= REL_MAX_DIST and DS <= page
    assert MOBA_BLOCK % page == 0 and n_pages % pps == 0 and pps % ppb == 0 and nbp <= LANES and page == LANES
    ns = n_pages // pps
    R = H * DS
    BW = H * dh
    rb_rows = jnp.repeat(rel_bias.astype(F32).T, DS, axis=0)[:, None, :]
    hq_q = jnp.tile(jnp.arange(DS), H)
    b_own = _bias_lookup(jnp.maximum(hq_q[:, None] - hq_q[None, :], 0), rb_rows)
    b_last = _bias_lookup(page + hq_q[:, None] - jnp.arange(page)[None, :], rb_rows)
    b_far = jnp.broadcast_to(rb_rows[:, :, REL_BUCKETS - 1], (R, LANES))
    ck = cache_k.reshape(cache_k.shape[0], n_pool, page * H, dh)
    cv = cache_v.reshape(cache_v.shape[0], n_pool, page * H, dh)

    def kspec(i):
        return pl.BlockSpec((1, 1, page * H, dh),
                            lambda b, s, pt: (layer, pt[b, jnp.minimum(s, ns - 1) * pps + i], 0, 0))

    def vspec(i):
        return pl.BlockSpec((1, 1, page * H, dh),
                            lambda b, s, pt: (layer, pt[b, jnp.maximum(s - ns, 0) * pps + i], 0, 0))

    def slot(sl):
        return pl.BlockSpec((1, DS, BW), lambda b, s, pt: (b, 0, sl))

    const = lambda a: pl.BlockSpec(a.shape, lambda b, s, pt: (0,) * a.ndim)
    return pl.pallas_call(
        functools.partial(_moba_sample_kernel, pps=pps, ppb=ppb, ns=ns, nbp=nbp, H=H, scale=dh ** -0.5,
                          inv_blk=1.0 / MOBA_BLOCK),
        grid_spec=pltpu.PrefetchScalarGridSpec(
            num_scalar_prefetch=1,
            grid=(DB, 2 * ns),
            in_specs=[slot(4), slot(5), slot(6), const(b_own), const(b_last), const(b_far)]
                     + [kspec(i) for i in range(pps)] + [vspec(i) for i in range(pps)],
            out_specs=pl.BlockSpec((1, DS, BW), lambda b, s, pt: (b, 0, 0)),
            scratch_shapes=[pltpu.VMEM((R, dh), F32), pltpu.VMEM((n_pages, R, page), F32),
                            pltpu.VMEM((R, LANES), F32), pltpu.VMEM((H, dh), F32),
                            pltpu.VMEM((R, 1), F32), pltpu.VMEM((R, dh), F32)]),
        out_shape=jax.ShapeDtypeStruct((DB, DS, BW), BF16),
        compiler_params=_params("parallel", "arbitrary"),
        name="moba_sample",
    )(page_table, proj, proj, proj, b_own, b_last, b_far, *([ck] * pps), *([cv] * pps))


def _conv_kernel(p7, p8, p9, p10, p11, scb, cfb, sw, cw, cb, nw, nbias, oc_ref, od_ref, scn_ref, cfn_ref,
                 extc, extd, ybuf, win, *, tl, nt, ws, wd):
    t = pl.program_id(1)
    P = CONV_PAD
    W = ybuf.shape[1]

    @pl.when(t == 0)
    def _():
        extc[pl.ds(P - (ws - 1), ws - 1), :] = scb[0]
        extd[pl.ds(P - (wd - 1), wd - 1), :] = cfb[0]

    extc[pl.ds(P, tl), :] = p8[0] * p9[0]
    extd[pl.ds(P, tl), :] = p10[0] * jax.nn.sigmoid(p11[0])

    yc = sw[0:1, :] * extc[pl.ds(P - (ws - 1), tl), :]
    for j in range(1, ws):
        yc = yc + sw[j:j + 1, :] * extc[pl.ds(P - (ws - 1) + j, tl), :]
    oc_ref[0] = (p7[0] * yc).astype(oc_ref.dtype)

    phases = min(SUBLANES, wd)
    for a in range(phases):
        rows = tl + (wd - 1 - a) // SUBLANES * SUBLANES
        win[a, pl.ds(0, rows), :] = extd[pl.ds(P - (wd - 1) + a, rows), :]
    for c0 in range(0, W, LANES):
        cols = slice(c0, c0 + LANES)
        acc = None
        for j in range(wd):
            a = j % phases
            term = cw[j:j + 1, cols] * win[a, pl.ds(j - a, tl), cols]
            acc = term if acc is None else acc + term
        ybuf[:, cols] = acc + cb[:, cols]
    y = ybuf[...]
    yc_ = y - jnp.mean(y, axis=-1, keepdims=True)
    yn = yc_ * lax.rsqrt(jnp.mean(yc_ * yc_, axis=-1, keepdims=True) + EPS) * nw[...] + nbias[...]
    od_ref[0] = (yn * jax.nn.sigmoid(yn)).astype(od_ref.dtype)

    newc = extc[pl.ds(P + tl - (ws - 1), ws - 1), :]
    newd = extd[pl.ds(P + tl - (wd - 1), wd - 1), :]
    extc[pl.ds(P - (ws - 1), ws - 1), :] = newc
    extd[pl.ds(P - (wd - 1), wd - 1), :] = newd

    @pl.when(t == nt - 1)
    def _():
        scn_ref[0] = newc
        cfn_ref[0] = newd


def conv_branches(proj, sc_buf, cf_buf, sconv_w, cconv_w, cconv_b, cnorm_w, cnorm_b):
    B, L, _ = proj.shape
    ws, BW = sconv_w.shape
    wd = cconv_w.shape[0]
    assert wd - 1 <= CONV_PAD
    tl = _pick(L, (128, 64, 32, 16, 8))
    nt = L // tl

    def slot(s):
        return pl.BlockSpec((1, tl, BW), lambda b, t: (b, t, s))

    full = lambda a: pl.BlockSpec(a.shape, lambda b, t: (0,) * a.ndim)
    cb2, nw2, nb2 = cconv_b.reshape(1, BW), cnorm_w.reshape(1, BW), cnorm_b.reshape(1, BW)
    ospec = pl.BlockSpec((1, tl, BW), lambda b, t: (b, t, 0))
    return pl.pallas_call(
        functools.partial(_conv_kernel, tl=tl, nt=nt, ws=ws, wd=wd),
        grid=(B, nt),
        in_specs=[slot(7), slot(8), slot(9), slot(10), slot(11),
                  pl.BlockSpec((1, ws - 1, BW), lambda b, t: (b, 0, 0)),
                  pl.BlockSpec((1, wd - 1, BW), lambda b, t: (b, 0, 0)),
                  full(sconv_w), full(cconv_w), full(cb2), full(nw2), full(nb2)],
        out_specs=[ospec, ospec,
                   pl.BlockSpec((1, ws - 1, BW), lambda b, t: (b, 0, 0)),
                   pl.BlockSpec((1, wd - 1, BW), lambda b, t: (b, 0, 0))],
        out_shape=[jax.ShapeDtypeStruct((B, L, BW), BF16), jax.ShapeDtypeStruct((B, L, BW), BF16),
                   jax.ShapeDtypeStruct((B, ws - 1, BW), F32), jax.ShapeDtypeStruct((B, wd - 1, BW), F32)],
        scratch_shapes=[pltpu.VMEM((CONV_PAD + tl, BW), F32), pltpu.VMEM((CONV_PAD + tl, BW), F32),
                        pltpu.VMEM((tl, BW), F32),
                        pltpu.VMEM((SUBLANES, tl + CONV_PAD - SUBLANES, BW), F32)],
        compiler_params=_params("parallel", "arbitrary"),
        name="conv_branches",
    )(proj, proj, proj, proj, proj, sc_buf, cf_buf, sconv_w, cconv_w, cb2, nw2, nb2)


def _branches(proj, pos, past, layer, ret_state, sc_buf, cf_buf, lw):
    (ret_gn_w, sconv_w, cconv_w, cconv_b, cnorm_w, cnorm_b, rel_bias) = lw
    B, L, W = proj.shape
    BW = W // N_IN_SLOTS
    H = BW // MOBA_HEAD_DIM
    o_a, ret_new = retention(proj, pos, ret_state, ret_gn_w)
    if past is None:
        o_b = moba_prompt(proj, rel_bias, H)
    else:
        cache_k, cache_v, page_table = past
        o_b = moba_sample(proj, cache_k, cache_v, layer, page_table, rel_bias)
    o_c, o_d, sc_new, cf_new = conv_branches(proj, sc_buf, cf_buf, sconv_w, cconv_w, cconv_b, cnorm_w, cnorm_b)
    k_b = proj[:, :, 5 * BW:6 * BW].reshape(B, L, H, MOBA_HEAD_DIM)
    v_b = proj[:, :, 6 * BW:7 * BW].reshape(B, L, H, MOBA_HEAD_DIM)
    return [a.reshape(B * L, BW) for a in (o_a, o_b, o_c, o_d)], (k_b, v_b, ret_new, sc_new, cf_new)


def kernel(x_prompt, x_sample, c_prompt, c_sample, cache_k, cache_v, state_ret, state_sconv, state_cconv, page_table, w_ada, b_ada, norm_g, w_in, w_gate, b_gate, ret_gn_w, sconv_w, cconv_w, cconv_b, cnorm_w, cnorm_b, w_branch, w_o, w_ffn_in, w_ffn_out, rel_bias):
    B, S, D = x_prompt.shape
    DB, DS, _ = x_sample.shape
    depth = w_in.shape[0]
    BW = D // N_BRANCH
    H = BW // MOBA_HEAD_DIM
    dh_ret = BW // RET_HEADS
    F = w_ffn_out.shape[1]
    page = cache_k.shape[2]
    past_len = page_table.shape[1] * page
    pos_p = jnp.arange(S, dtype=jnp.int32)
    pos_s = past_len + jnp.arange(DS, dtype=jnp.int32)
    Mp, Ms = B * S, DB * DS

    rows = B + DB
    rows_pad = -(-rows // 8) * 8
    c_all = jnp.concatenate([c_prompt, c_sample, jnp.zeros((rows_pad - rows, D), F32)], axis=0)
    mods = ada_mod(c_all, w_ada, b_ada).reshape(depth, rows_pad * N_MOD, 1, D)
    w_ffn_out_b = w_ffn_out.astype(BF16)

    xp, xs = x_prompt, x_sample
    hp = norm_mod(xp, norm_g[0, 0], mods[0], 0, (1, 0))
    hs = norm_mod(xs, norm_g[0, 0], mods[0], B, (1, 0))
    zeros_ret = jnp.zeros((B, RET_HEADS, dh_ret, dh_ret), F32)
    zeros_sc = jnp.zeros((B, sconv_w.shape[1] - 1, BW), F32)
    zeros_cf = jnp.zeros((B, cconv_w.shape[1] - 1, BW), F32)
    outs_p, outs_s = [], []
    for l in range(depth):
        mod = mods[l]
        lw = (ret_gn_w[l], sconv_w[l], cconv_w[l], cconv_b[l], cnorm_w[l], cnorm_b[l], rel_bias)
        nxt = (norm_g[l + 1, 0], mods[l + 1], 1, 0) if l + 1 < depth else None
        hp2, hs2 = hp.reshape(Mp, D), hs.reshape(Ms, D)
        proj_p, proj_s = matmul(hp2, hs2, w_in, l, tm=2048, tn=512)
        br_p, op = _branches(proj_p.reshape(B, S, -1), pos_p, None, l, zeros_ret, zeros_sc, zeros_cf, lw)
        br_s, os_ = _branches(proj_s.reshape(DB, DS, -1), pos_s, (cache_k, cache_v, page_table), l,
                              state_ret[l], state_sconv[l], state_cconv[l], lw)
        mg_p, mg_s = gate_merge(hp2, hs2, br_p, br_s, w_gate, b_gate[l], w_branch, l)
        y_p, y_s = matmul(mg_p, mg_s, w_o, l, tm=2048, tn=512, out_dtype=BF16)
        ffn_mod = (norm_g[l, 2], mod, 4, 3)
        xp, hp = resid_norm(xp, y_p.reshape(B, S, D), norm_g[l, 1], mod, 0, 2, nxt=ffn_mod)
        xs, hs = resid_norm(xs, y_s.reshape(DB, DS, D), norm_g[l, 1], mod, B, 2, nxt=ffn_mod)
        act_p, act_s = ffn_in(hp.reshape(Mp, D), hs.reshape(Ms, D), w_ffn_in, l)
        y_p, y_s = matmul(act_p, act_s, w_ffn_out_b, l, tm=1024, tn=512, tk=F // 2)
        xp, hp = resid_norm(xp, y_p.reshape(B, S, D), norm_g[l, 3], mod, 0, 5, nxt=nxt)
        xs, hs = resid_norm(xs, y_s.reshape(DB, DS, D), norm_g[l, 3], mod, B, 5, nxt=nxt)
        outs_p.append(op)
        outs_s.append(os_)
    stack = lambda outs, i: jnp.stack([o[i] for o in outs])
    new_k_prompt = stack(outs_p, 0).reshape(depth, B, S // page, page, H, MOBA_HEAD_DIM)
    new_v_prompt = stack(outs_p, 1).reshape(depth, B, S // page, page, H, MOBA_HEAD_DIM)
    return (xp, xs, new_k_prompt, new_v_prompt, stack(outs_s, 0), stack(outs_s, 1),
            stack(outs_p, 2), stack(outs_s, 2), stack(outs_p, 3), stack(outs_s, 3),
            stack(outs_p, 4), stack(outs_s, 4))
```

```python
import functools
import math

import jax
import jax.numpy as jnp
from jax import lax
from jax.experimental import pallas as pl
from jax.experimental.pallas import tpu as pltpu

F32 = jnp.float32
BF16 = jnp.bfloat16

N_BRANCH = 4
N_IN_SLOTS = 12
RET_HEADS = 4
RET_CHUNK = 128
ROPE_BASE = 10000.0
MOBA_HEAD_DIM = 128
MOBA_BLOCK = 256
MOBA_TOPK = 3
REL_BUCKETS = 32
REL_MAX_DIST = 128
N_MOD = 6
EPS = 1e-6
GN_EPS = 1e-5
CONV_PAD = 32
LANES = 128
SUBLANES = 8
MXU_COLS = 256
KV_PAGES_PER_STEP = 16

VMEM_LIMIT_BYTES = 56 * 1024 * 1024


def _params(*sem):
    return pltpu.CompilerParams(dimension_semantics=sem, vmem_limit_bytes=VMEM_LIMIT_BYTES)


def _pick(dim, prefs):
    for p in prefs:
        if dim % p == 0:
            return p
    return dim


def _bdot(a, b):
    return jnp.dot(a.astype(BF16), b.astype(BF16), preferred_element_type=F32)


def _bdot_nt(a, b):
    return lax.dot_general(a.astype(BF16), b.astype(BF16), (((1,), (1,)), ((), ())), preferred_element_type=F32)


def _bdot_tn(a, b):
    return lax.dot_general(a.astype(BF16), b.astype(BF16), (((0,), (0,)), ((), ())), preferred_element_type=F32)


def _fdot_nt(a, b):
    return lax.dot_general(a, b, (((1,), (1,)), ((), ())), precision=lax.Precision.HIGHEST, preferred_element_type=F32)


def _resident(shape, imap):
    return pl.BlockSpec(shape, imap, pipeline_mode=pl.Buffered(1))


def _mm_kernel(xp_ref, xs_ref, w_ref, op_ref, os_ref, *, nk):
    i = pl.program_id(0)
    if nk == 1:
        for c0 in range(0, w_ref.shape[1], MXU_COLS):
            cols = slice(c0, c0 + MXU_COLS)
            w = w_ref[:, cols].astype(BF16)
            op_ref[:, cols] = jnp.dot(xp_ref[...], w, preferred_element_type=F32).astype(op_ref.dtype)

        @pl.when(i == 0)
        def _():
            w = w_ref[...].astype(BF16)
            os_ref[0] = jnp.dot(xs_ref[...], w, preferred_element_type=F32).astype(os_ref.dtype)
    else:
        w = w_ref[...].astype(BF16)
        part = jnp.dot(xp_ref[...], w, preferred_element_type=F32)
        k = pl.program_id(2)

        @pl.when(k == 0)
        def _():
            op_ref[...] = part

        @pl.when(k > 0)
        def _():
            op_ref[...] += part

        @pl.when((i == 0) & (k == 0))
        def _():
            os_ref[0] = jnp.dot(xs_ref[...], w, preferred_element_type=F32)

        @pl.when((i == 0) & (k > 0))
        def _():
            os_ref[0] += jnp.dot(xs_ref[...], w, preferred_element_type=F32)


def _unblock(o_s):
    nj, Ms, tn = o_s.shape
    return o_s.transpose(1, 0, 2).reshape(Ms, nj * tn)


def matmul(xp, xs, w, layer, *, tm, tn, tk=None, out_dtype=F32):
    Mp, K = xp.shape
    Ms = xs.shape[0]
    N = w.shape[2]
    tm = min(tm, Mp)
    tk = tk or K
    nk = K // tk
    nj = N // tn
    assert Mp % tm == 0 and N % tn == 0 and K % tk == 0 and tn % MXU_COLS == 0
    assert nk == 1 or out_dtype == F32
    xp_spec = (_resident((tm, tk), lambda i, j, k: (i, k)) if nk == 1
               else pl.BlockSpec((tm, tk), lambda i, j, k: (i, k)))
    o_p, o_s = pl.pallas_call(
        functools.partial(_mm_kernel, nk=nk),
        grid=(Mp // tm, nj, nk),
        in_specs=[xp_spec,
                  pl.BlockSpec((Ms, tk), lambda i, j, k: (0, k)),
                  pl.BlockSpec((None, tk, tn), lambda i, j, k: (layer, k, j))],
        out_specs=[pl.BlockSpec((tm, tn), lambda i, j, k: (i, j)),
                   pl.BlockSpec((1, Ms, tn), lambda i, j, k: (jnp.where(i == 0, j, nj - 1), 0, 0))],
        out_shape=[jax.ShapeDtypeStruct((Mp, N), out_dtype), jax.ShapeDtypeStruct((nj, Ms, tn), out_dtype)],
        compiler_params=_params("arbitrary", "arbitrary", "arbitrary"),
        name="matmul",
    )(xp, xs, w)
    return o_p, _unblock(o_s)


def _gate_merge_kernel(hp_ref, hs_ref, *refs):
    brp = refs[0:4]
    brs = refs[4:8]
    wg_ref, wb_ref, bg_ref, op_ref, os_ref, accp, accs = refs[8:]
    i = pl.program_id(0)
    n = pl.program_id(2)

    def term(h_ref, br_ref, cols):
        wg = wg_ref[:, cols].astype(BF16)
        wb = wb_ref[:, cols].astype(BF16)
        gate = jax.nn.sigmoid(jnp.dot(h_ref[...], wg, preferred_element_type=F32) + bg_ref[0][:, cols])
        return gate * jnp.dot(br_ref[...], wb, preferred_element_type=F32)

    for b in range(N_BRANCH):
        @pl.when(n == b)
        def _(b=b):
            for c0 in range(0, op_ref.shape[1], MXU_COLS):
                cols = slice(c0, c0 + MXU_COLS)
                t = term(hp_ref, brp[b], cols)
                if b == 0:
                    accp[:, cols] = t
                elif b < N_BRANCH - 1:
                    accp[:, cols] += t
                else:
                    op_ref[:, cols] = (accp[:, cols] + t).astype(op_ref.dtype)

        @pl.when((n == b) & (i == 0))
        def _(b=b):
            t = term(hs_ref, brs[b], slice(None))
            if b == 0:
                accs[...] = t
            elif b < N_BRANCH - 1:
                accs[...] += t
            else:
                os_ref[0] = (accs[...] + t).astype(os_ref.dtype)


def gate_merge(hp, hs, branches_p, branches_s, w_gate, b_gate, w_branch, layer, *, tm=2048, tn=256):
    Mp, D = hp.shape
    Ms = hs.shape[0]
    BW = branches_p[0].shape[1]
    tm = min(tm, Mp)
    nj = D // tn
    assert Mp % tm == 0 and D % tn == 0
    bg3 = b_gate.reshape(N_BRANCH, 1, D)
    in_specs = [_resident((tm, D), lambda i, j, n: (i, 0)),
                pl.BlockSpec((Ms, D), lambda i, j, n: (0, 0))]
    in_specs += [_resident((tm, BW), lambda i, j, n: (i, 0)) for _ in range(N_BRANCH)]
    in_specs += [pl.BlockSpec((Ms, BW), lambda i, j, n: (0, 0)) for _ in range(N_BRANCH)]
    in_specs += [pl.BlockSpec((None, D, tn), lambda i, j, n: (layer, 0, n * nj + j)),
                 pl.BlockSpec((None, None, BW, tn), lambda i, j, n: (layer, n, 0, j)),
                 pl.BlockSpec((1, 1, tn), lambda i, j, n: (n, 0, j))]
    o_p, o_s = pl.pallas_call(
        _gate_merge_kernel,
        grid=(Mp // tm, nj, N_BRANCH),
        in_specs=in_specs,
        out_specs=[pl.BlockSpec((tm, tn), lambda i, j, n: (i, j)),
                   pl.BlockSpec((1, Ms, tn), lambda i, j, n: (jnp.where(i == 0, j, nj - 1), 0, 0))],
        out_shape=[jax.ShapeDtypeStruct((Mp, D), BF16), jax.ShapeDtypeStruct((nj, Ms, tn), BF16)],
        scratch_shapes=[pltpu.VMEM((tm, tn), F32), pltpu.VMEM((Ms, tn), F32)],
        compiler_params=_params("arbitrary", "arbitrary", "arbitrary"),
        name="gate_merge",
    )(hp, hs, *branches_p, *branches_s, w_gate, w_branch, bg3)
    return o_p, _unblock(o_s)


def _ffn_in_kernel(hp_ref, hs_ref, wa_ref, wb_ref, wo_ref, op_ref, os_ref, wob_ref):
    wob_ref[...] = wo_ref[...].astype(BF16)
    wa = wa_ref[...].astype(BF16)
    wb = wb_ref[...].astype(BF16)

    def act(h_ref):
        a = jnp.dot(h_ref[...], wa, preferred_element_type=F32)
        b = jnp.dot(h_ref[...], wb, preferred_element_type=F32)
        return a * jax.nn.sigmoid(a) * b

    op_ref[...] = act(hp_ref).astype(op_ref.dtype)

    @pl.when(pl.program_id(0) == 0)
    def _():
        os_ref[0] = act(hs_ref).astype(os_ref.dtype)


def ffn_in(hp, hs, w, w_out, layer, *, tm=2048, tn=256):
    Mp, D = hp.shape
    Ms = hs.shape[0]
    F = w.shape[2] // 2
    Dout = w_out.shape[2]
    tm = min(tm, Mp)
    ni, nj = Mp // tm, F // tn
    slab = F // (ni * nj)
    assert Mp % tm == 0 and F % tn == 0 and F % (ni * nj) == 0 and slab % 16 == 0
    o_p, o_s, w_out_b = pl.pallas_call(
        _ffn_in_kernel,
        grid=(ni, nj),
        in_specs=[_resident((tm, D), lambda i, j: (i, 0)),
                  pl.BlockSpec((Ms, D), lambda i, j: (0, 0)),
                  pl.BlockSpec((None, D, tn), lambda i, j: (layer, 0, j)),
                  pl.BlockSpec((None, D, tn), lambda i, j: (layer, 0, nj + j)),
                  pl.BlockSpec((None, slab, Dout), lambda i, j: (layer, i * nj + j, 0))],
        out_specs=[pl.BlockSpec((tm, tn), lambda i, j: (i, j)),
                   pl.BlockSpec((1, Ms, tn), lambda i, j: (jnp.where(i == 0, j, nj - 1), 0, 0)),
                   pl.BlockSpec((slab, Dout), lambda i, j: (i * nj + j, 0))],
        out_shape=[jax.ShapeDtypeStruct((Mp, F), BF16), jax.ShapeDtypeStruct((nj, Ms, tn), BF16),
                   jax.ShapeDtypeStruct((F, Dout), BF16)],
        compiler_params=_params("arbitrary", "arbitrary"),
        name="ffn_in",
    )(hp, hs, w, w, w_out)
    return o_p, _unblock(o_s), w_out_b


def _ada_kernel(c_ref, w_ref, b_ref, o_ref):
    c = c_ref[...]
    x = c * jax.nn.sigmoid(c)
    o_ref[0] = _bdot(x, w_ref[0]) + b_ref[0]


def ada_mod(c_all, w_ada, b_ada):
    depth, D, N = w_ada.shape
    R = c_all.shape[0]
    tn = 512
    return pl.pallas_call(
        _ada_kernel,
        grid=(depth, N // tn),
        in_specs=[pl.BlockSpec((R, D), lambda l, j: (0, 0)),
                  pl.BlockSpec((1, D, tn), lambda l, j: (l, 0, j)),
                  pl.BlockSpec((1, 1, tn), lambda l, j: (l, 0, j))],
        out_specs=pl.BlockSpec((1, R, tn), lambda l, j: (l, 0, j)),
        out_shape=jax.ShapeDtypeStruct((depth, R, N), F32),
        compiler_params=_params("parallel", "parallel"),
        name="ada_mod",
    )(c_all, w_ada, b_ada.reshape(depth, 1, N))


def _rms(x, g):
    return x * lax.rsqrt(jnp.mean(x * x, axis=-1, keepdims=True) + EPS) * g


def _norm_mod_kernel(x_ref, g_ref, sc_ref, sh_ref, h_ref):
    h = _rms(x_ref[0], g_ref[...]) * (1.0 + sc_ref[0]) + sh_ref[0]
    h_ref[0] = h.astype(h_ref.dtype)


def _mod_spec(D, row0, slot):
    return pl.BlockSpec((1, 1, D), lambda b, t: ((row0 + b) * N_MOD + slot, 0, 0))


def norm_mod(x, g, mod, row0, slots):
    B, L, D = x.shape
    tl = _pick(L, (256, 128, 64, 32, 16, 8))
    return pl.pallas_call(
        _norm_mod_kernel,
        grid=(B, L // tl),
        in_specs=[pl.BlockSpec((1, tl, D), lambda b, t: (b, t, 0)),
                  pl.BlockSpec((1, D), lambda b, t: (0, 0)),
                  _mod_spec(D, row0, slots[0]), _mod_spec(D, row0, slots[1])],
        out_specs=pl.BlockSpec((1, tl, D), lambda b, t: (b, t, 0)),
        out_shape=jax.ShapeDtypeStruct((B, L, D), BF16),
        compiler_params=_params("parallel", "parallel"),
        name="norm_mod",
    )(x, g.reshape(1, D), mod, mod)


def _resid_kernel(x_ref, y_ref, ga_ref, gate_ref, *rest, emit_h):
    xn = x_ref[0] + gate_ref[0] * _rms(y_ref[0].astype(F32), ga_ref[...])
    if emit_h:
        gb_ref, sc_ref, sh_ref, xo_ref, h_ref = rest
        xo_ref[0] = xn
        h_ref[0] = (_rms(xn, gb_ref[...]) * (1.0 + sc_ref[0]) + sh_ref[0]).astype(h_ref.dtype)
    else:
        (xo_ref,) = rest
        xo_ref[0] = xn


def resid_norm(x, y, g_a, mod, row0, gate_slot, nxt=None):
    B, L, D = x.shape
    tl = _pick(L, (256, 128, 64, 32, 16, 8))
    xspec = pl.BlockSpec((1, tl, D), lambda b, t: (b, t, 0))
    gspec = pl.BlockSpec((1, D), lambda b, t: (0, 0))
    in_specs = [xspec, xspec, gspec, _mod_spec(D, row0, gate_slot)]
    args = [x, y, g_a.reshape(1, D), mod]
    out_specs = [xspec]
    out_shape = [jax.ShapeDtypeStruct((B, L, D), F32)]
    if nxt is not None:
        g_b, mod_n, sc_slot, sh_slot = nxt
        in_specs += [gspec, _mod_spec(D, row0, sc_slot), _mod_spec(D, row0, sh_slot)]
        args += [g_b.reshape(1, D), mod_n, mod_n]
        out_specs.append(xspec)
        out_shape.append(jax.ShapeDtypeStruct((B, L, D), BF16))
    res = pl.pallas_call(
        functools.partial(_resid_kernel, emit_h=nxt is not None),
        grid=(B, L // tl),
        in_specs=in_specs, out_specs=out_specs, out_shape=out_shape,
        compiler_params=_params("parallel", "parallel"),
        name="resid_norm",
    )(*args)
    return (res[0], res[1]) if nxt is not None else (res[0], None)


def _ret_kernel(q_ref, k_ref, v_ref, g_ref, cos_ref, sin_ref, dec_ref, qd_ref, kd_ref, gl_ref, s0_ref, gnw_ref,
                o_ref, so_ref, s_scr, *, nc, dh):
    c = pl.program_id(1)

    @pl.when(c == 0)
    def _():
        s_scr[...] = s0_ref[0]

    half = dh // 2
    cos = cos_ref[...]
    sin = sin_ref[...]

    def rot(x):
        x1 = x[:, :half]
        x2 = x[:, half:]
        return jnp.concatenate([x1 * cos - x2 * sin, x1 * sin + x2 * cos], axis=-1)

    for h in range(s_scr.shape[0]):
        cols = slice(h * dh, (h + 1) * dh)
        q = rot(q_ref[0, :, cols])
        k = rot(k_ref[0, :, cols]) * (dh ** -0.5)
        v = v_ref[0, :, cols]
        S = s_scr[h]
        scores = _bdot_nt(q, k) * dec_ref[h]
        o = _bdot(scores, v) + _bdot(q * qd_ref[h], S)
        s_scr[h] = gl_ref[h, 0:1, :] * S + _bdot_tn(k * kd_ref[h], v)
        oc = o - jnp.mean(o, axis=-1, keepdims=True)
        on = oc * lax.rsqrt(jnp.mean(oc * oc, axis=-1, keepdims=True) + GN_EPS)
        g = g_ref[0, :, cols]
        o_ref[0, :, cols] = (on * gnw_ref[:, cols] * (g * jax.nn.sigmoid(g))).astype(o_ref.dtype)

    @pl.when(c == nc - 1)
    def _():
        so_ref[0] = s_scr[...]


def retention(proj, pos, state0, gn_w):
    B, L, _ = proj.shape
    H = RET_HEADS
    BW = gn_w.shape[0]
    dh = BW // H
    half = dh // 2
    C = RET_CHUNK if L % RET_CHUNK == 0 else L
    nc = L // C
    inv = jnp.exp(-math.log(ROPE_BASE) * jnp.arange(half, dtype=F32) / half)
    ang = pos.astype(F32)[:, None] * inv[None, :]
    cos, sin = jnp.cos(ang), jnp.sin(ang)
    log_g = jnp.log1p(-jnp.exp2(-5.0 - jnp.arange(H, dtype=F32)))
    idx = jnp.arange(C, dtype=F32)
    diff = idx[:, None] - idx[None, :]
    causal = diff >= 0
    dec = jnp.where(causal[None], jnp.exp(log_g[:, None, None] * jnp.where(causal, diff, 0.0)[None]), 0.0)
    qd = jnp.broadcast_to(jnp.exp(log_g[:, None] * (idx[None, :] + 1.0))[:, :, None], (H, C, dh))
    kd = jnp.broadcast_to(jnp.exp(log_g[:, None] * ((C - 1.0) - idx[None, :]))[:, :, None], (H, C, dh))
    gl = jnp.broadcast_to(jnp.exp(log_g * C)[:, None, None], (H, 8, dh))

    def slot(s):
        return pl.BlockSpec((1, C, BW), lambda b, c: (b, c, s))

    tab = pl.BlockSpec((C, half), lambda b, c: (c, 0))
    const = lambda a: pl.BlockSpec(a.shape, lambda b, c: (0,) * a.ndim)
    return pl.pallas_call(
        functools.partial(_ret_kernel, nc=nc, dh=dh),
        grid=(B, nc),
        in_specs=[slot(0), slot(1), slot(2), slot(3), tab, tab, const(dec), const(qd), const(kd), const(gl),
                  pl.BlockSpec((1, H, dh, dh), lambda b, c: (b, 0, 0, 0)),
                  pl.BlockSpec((1, BW), lambda b, c: (0, 0))],
        out_specs=[pl.BlockSpec((1, C, BW), lambda b, c: (b, c, 0)),
                   pl.BlockSpec((1, H, dh, dh), lambda b, c: (b, 0, 0, 0))],
        out_shape=[jax.ShapeDtypeStruct((B, L, BW), BF16),
                   jax.ShapeDtypeStruct((B, H, dh, dh), F32)],
        scratch_shapes=[pltpu.VMEM((H, dh, dh), F32)],
        compiler_params=_params("parallel", "arbitrary"),
        name="retention",
    )(proj, proj, proj, proj, cos, sin, dec, qd, kd, gl, state0, gn_w.reshape(1, BW))


def t5_bucket(n):
    max_exact = REL_BUCKETS // 2
    nf = jnp.maximum(n, 1).astype(F32)
    large = max_exact + (jnp.log(nf / max_exact) / math.log(REL_MAX_DIST / max_exact) * (REL_BUCKETS - max_exact)).astype(jnp.int32)
    large = jnp.minimum(large, REL_BUCKETS - 1)
    return jnp.where(n < max_exact, n, large)


def _bias_lookup(dist, rb_rows):
    k = jnp.arange(REL_BUCKETS)
    return jnp.sum(jnp.where(t5_bucket(dist)[..., None] == k, rb_rows, 0.0), axis=-1)


def _select_blocks(gate, n_elig, nb):
    col = lax.broadcasted_iota(jnp.int32, gate.shape, 1)
    rank = jnp.zeros(gate.shape, jnp.int32)
    for m in range(nb):
        gm = gate[:, m:m + 1]
        beats = (gm > gate) | ((gm == gate) & (m < col))
        rank = rank + jnp.where(beats, jnp.where(m < n_elig, 1, 0), 0)
    return (col < n_elig) & (rank < MOBA_TOPK)


def _moba_prompt_kernel(q_ref, k_ref, v_ref, tb_ref, o_ref, kb_scr, vb_scr, *, nb, blk, scale):
    L = nb * blk
    kb_scr[...] = k_ref[0].astype(BF16)
    vb_scr[...] = v_ref[0].astype(BF16)
    kbar = jnp.concatenate(
        [jnp.mean(k_ref[0, j * blk:(j + 1) * blk, :], axis=0, keepdims=True) for j in range(nb)], axis=0)
    q = q_ref[0]
    gate = _fdot_nt(kbar, q)
    n_idx = lax.broadcasted_iota(jnp.int32, (nb, L), 0)
    own = lax.broadcasted_iota(jnp.int32, (nb, L), 1) // blk
    rank = jnp.zeros((nb, L), jnp.int32)
    for m in range(nb):
        gm = gate[m:m + 1, :]
        beats = (gm > gate) | ((gm == gate) & (m < n_idx))
        rank = rank + jnp.where(beats & (m < own), 1, 0)
    sel = jnp.where((n_idx < own) & (rank < MOBA_TOPK), 1.0, 0.0).T

    r = lax.broadcasted_iota(jnp.int32, (blk, blk), 0)
    c = lax.broadcasted_iota(jnp.int32, (blk, blk), 1)
    causal = c <= r
    far_bias = tb_ref[0, 0:1, 0:1]
    for i in range(nb):
        rows = slice(i * blk, (i + 1) * blk)
        s = _bdot_nt(q[rows] * scale, kb_scr[0:(i + 1) * blk, :])
        parts = []
        for j in range(i + 1):
            sj = s[:, j * blk:(j + 1) * blk]
            if j == i:
                parts.append(jnp.where(causal, sj + tb_ref[0, :, blk:], -jnp.inf))
            else:
                bias = tb_ref[0, :, :blk] if j == i - 1 else far_bias
                parts.append(jnp.where(sel[rows, j:j + 1] > 0.0, sj + bias, -jnp.inf))
        s = jnp.concatenate(parts, axis=1) if len(parts) > 1 else parts[0]
        m = jnp.max(s, axis=-1, keepdims=True)
        p = jnp.exp(s - m)
        l = jnp.sum(p, axis=-1, keepdims=True)
        o_ref[0, rows, :] = (_bdot(p, vb_scr[0:(i + 1) * blk, :]) / l).astype(o_ref.dtype)


def moba_prompt(proj, rel_bias, H):
    B, L, _ = proj.shape
    blk = MOBA_BLOCK
    dh = MOBA_HEAD_DIM
    nb = L // blk
    assert L % blk == 0 and blk > REL_MAX_DIST
    P = 2 * blk + 1
    mm = jnp.arange(P)
    dvec = jnp.where(mm <= blk, blk - mm, 2 * blk - 1)
    bvec = _bias_lookup(dvec[None, :], rel_bias.astype(F32).T[:, None, :])
    tb = jnp.tile(bvec, (1, blk))[:, :blk * (P - 1)].reshape(H, blk, P - 1)
    return pl.pallas_call(
        functools.partial(_moba_prompt_kernel, nb=nb, blk=blk, scale=dh ** -0.5),
        grid=(B, H),
        in_specs=[pl.BlockSpec((1, L, dh), lambda b, h: (b, 0, 4 * H + h)),
                  pl.BlockSpec((1, L, dh), lambda b, h: (b, 0, 5 * H + h)),
                  pl.BlockSpec((1, L, dh), lambda b, h: (b, 0, 6 * H + h)),
                  pl.BlockSpec((1, blk, 2 * blk), lambda b, h: (h, 0, 0))],
        out_specs=pl.BlockSpec((1, L, dh), lambda b, h: (b, 0, h)),
        out_shape=jax.ShapeDtypeStruct((B, L, H * dh), BF16),
        scratch_shapes=[pltpu.VMEM((L, dh), BF16), pltpu.VMEM((L, dh), BF16)],
        compiler_params=_params("parallel", "parallel"),
        name="moba_prompt",
    )(proj, proj, proj, tb)


def _moba_sample_kernel(pt_ref, q_ref, kn_ref, vn_ref, bown_ref, blast_ref, bfar_ref, *refs,
                        pps, ppb, ns, nbp, H, scale, inv_blk):
    k_refs = refs[:pps]
    v_refs = refs[pps:2 * pps]
    o_ref = refs[2 * pps]
    q_scr, p_scr, gate_scr, ksum_scr, l_scr, acc_scr = refs[2 * pps + 1:]
    s = pl.program_id(1)
    ds = q_ref.shape[1]
    dh = q_scr.shape[1]
    R = H * ds
    page = k_refs[0].shape[2] // H
    n_pages = ns * pps

    def head_major(ref):
        return jnp.concatenate([ref[0, :, h * dh:(h + 1) * dh] for h in range(H)], axis=0)

    @pl.when(s == 0)
    def _():
        q_scr[...] = head_major(q_ref)
        gate_scr[...] = jnp.zeros_like(gate_scr)

    @pl.when(s < ns)
    def _():
        lane = lax.broadcasted_iota(jnp.int32, (ds, LANES), 1)
        for i in range(pps):
            pg = s * pps + i
            for h in range(H):
                rows = slice(h * ds, (h + 1) * ds)
                kh = k_refs[i][0, 0, pl.ds(h, page, stride=H), :]
                p_scr[pg, rows, :] = _bdot_nt(q_scr[rows, :], kh) * scale
                tot = jnp.sum(kh, axis=0, keepdims=True)
                if i % ppb == 0:
                    ksum_scr[h:h + 1, :] = tot
                else:
                    ksum_scr[h:h + 1, :] += tot
                if i % ppb == ppb - 1:
                    g = jnp.sum(q_scr[rows, :] * (ksum_scr[h:h + 1, :] * inv_blk), axis=-1, keepdims=True)
                    gate_scr[rows, :] += jnp.where(lane == pg // ppb, g, 0.0)

    @pl.when(s == ns)
    def _():
        sel = jnp.where(_select_blocks(gate_scr[...], nbp, nbp), 1.0, 0.0)
        far = bfar_ref[:, 0:1]
        q = q_scr[...]
        kn = head_major(kn_ref)
        vn = head_major(vn_ref)
        so = _bdot_nt(q, kn) * scale + bown_ref[...]
        r = lax.broadcasted_iota(jnp.int32, (R, R), 0)
        c = lax.broadcasted_iota(jnp.int32, (R, R), 1)
        so = jnp.where((c // ds == r // ds) & (c % ds <= r % ds), so, -jnp.inf)
        m_own = jnp.max(so, axis=-1, keepdims=True)

        picked = [jnp.broadcast_to(sel[:, n:n + 1] > 0.0, (R, page)) for n in range(nbp)]

        def masked(pg):
            return jnp.where(picked[pg // ppb], p_scr[pg], -jnp.inf)

        mx = masked(0)
        for pg in range(1, n_pages - 1):
            mx = jnp.maximum(mx, masked(pg))
        last = masked(n_pages - 1) + blast_ref[...]
        m = jnp.maximum(jnp.max(jnp.maximum(mx + far, last), axis=-1, keepdims=True), m_own)
        shift = far - m
        lsum = jnp.exp(last - m)
        p_scr[n_pages - 1] = lsum
        for pg in range(n_pages - 1):
            p = jnp.exp(masked(pg) + shift)
            p_scr[pg] = p
            lsum = lsum + p
        p_own = jnp.exp(so - m)
        l_scr[...] = jnp.sum(lsum, axis=-1, keepdims=True) + jnp.sum(p_own, axis=-1, keepdims=True)
        acc_scr[...] = _bdot(p_own, vn)

    @pl.when(s >= ns)
    def _():
        for i in range(pps):
            pg = (s - ns) * pps + i
            for h in range(H):
                rows = slice(h * ds, (h + 1) * ds)
                vh = v_refs[i][0, 0, pl.ds(h, page, stride=H), :]
                acc_scr[rows, :] += _bdot(p_scr[pg, rows, :], vh)

        @pl.when(s == 2 * ns - 1)
        def _():
            o = acc_scr[...] / l_scr[...]
            for h in range(H):
                o_ref[0, :, h * dh:(h + 1) * dh] = o[h * ds:(h + 1) * ds, :].astype(o_ref.dtype)


def moba_sample(proj, cache_k, cache_v, layer, page_table, rel_bias):
    DB, DS, _ = proj.shape
    _, n_pool, page, H, dh = cache_k.shape
    n_pages = page_table.shape[1]
    past_len = n_pages * page
    ppb = MOBA_BLOCK // page
    pps = min(KV_PAGES_PER_STEP, n_pages)
    nbp = past_len // MOBA_BLOCK
    assert past_len % MOBA_BLOCK == 0 and DS <= MOBA_BLOCK and page >= REL_MAX_DIST and DS <= page
    assert MOBA_BLOCK % page == 0 and n_pages % pps == 0 and pps % ppb == 0 and nbp <= LANES and page == LANES
    ns = n_pages // pps
    R = H * DS
    BW = H * dh
    rb_rows = jnp.repeat(rel_bias.astype(F32).T, DS, axis=0)[:, None, :]
    hq_q = jnp.tile(jnp.arange(DS), H)
    b_own = _bias_lookup(jnp.maximum(hq_q[:, None] - hq_q[None, :], 0), rb_rows)
    b_last = _bias_lookup(page + hq_q[:, None] - jnp.arange(page)[None, :], rb_rows)
    b_far = jnp.broadcast_to(rb_rows[:, :, REL_BUCKETS - 1], (R, LANES))
    ck = cache_k.reshape(cache_k.shape[0], n_pool, page * H, dh)
    cv = cache_v.reshape(cache_v.shape[0], n_pool, page * H, dh)

    def kspec(i):
        return pl.BlockSpec((1, 1, page * H, dh),
                            lambda b, s, pt: (layer, pt[b, jnp.minimum(s, ns - 1) * pps + i], 0, 0))

    def vspec(i):
        return pl.BlockSpec((1, 1, page * H, dh),
                            lambda b, s, pt: (layer, pt[b, jnp.maximum(s - ns, 0) * pps + i], 0, 0))

    def slot(sl):
        return pl.BlockSpec((1, DS, BW), lambda b, s, pt: (b, 0, sl))

    const = lambda a: pl.BlockSpec(a.shape, lambda b, s, pt: (0,) * a.ndim)
    return pl.pallas_call(
        functools.partial(_moba_sample_kernel, pps=pps, ppb=ppb, ns=ns, nbp=nbp, H=H, scale=dh ** -0.5,
                          inv_blk=1.0 / MOBA_BLOCK),
        grid_spec=pltpu.PrefetchScalarGridSpec(
            num_scalar_prefetch=1,
            grid=(DB, 2 * ns),
            in_specs=[slot(4), slot(5), slot(6), const(b_own), const(b_last), const(b_far)]
                     + [kspec(i) for i in range(pps)] + [vspec(i) for i in range(pps)],
            out_specs=pl.BlockSpec((1, DS, BW), lambda b, s, pt: (b, 0, 0)),
            scratch_shapes=[pltpu.VMEM((R, dh), F32), pltpu.VMEM((n_pages, R, page), F32),
                            pltpu.VMEM((R, LANES), F32), pltpu.VMEM((H, dh), F32),
                            pltpu.VMEM((R, 1), F32), pltpu.VMEM((R, dh), F32)]),
        out_shape=jax.ShapeDtypeStruct((DB, DS, BW), BF16),
        compiler_params=_params("parallel", "arbitrary"),
        name="moba_sample",
    )(page_table, proj, proj, proj, b_own, b_last, b_far, *([ck] * pps), *([cv] * pps))


def _conv_kernel(p7, p8, p9, p10, p11, scb, cfb, sw, cw, cb, nw, nbias, oc_ref, od_ref, scn_ref, cfn_ref,
                 extc, extd, ybuf, win, *, tl, nt, ws, wd):
    t = pl.program_id(1)
    P = CONV_PAD
    W = ybuf.shape[1]

    @pl.when(t == 0)
    def _():
        extc[pl.ds(P - (ws - 1), ws - 1), :] = scb[0]
        extd[pl.ds(P - (wd - 1), wd - 1), :] = cfb[0]

    extc[pl.ds(P, tl), :] = p8[0] * p9[0]
    extd[pl.ds(P, tl), :] = p10[0] * jax.nn.sigmoid(p11[0])

    yc = sw[0:1, :] * extc[pl.ds(P - (ws - 1), tl), :]
    for j in range(1, ws):
        yc = yc + sw[j:j + 1, :] * extc[pl.ds(P - (ws - 1) + j, tl), :]
    oc_ref[0] = (p7[0] * yc).astype(oc_ref.dtype)

    phases = min(SUBLANES, wd)
    for a in range(phases):
        rows = tl + (wd - 1 - a) // SUBLANES * SUBLANES
        win[a, pl.ds(0, rows), :] = extd[pl.ds(P - (wd - 1) + a, rows), :]
    for c0 in range(0, W, LANES):
        cols = slice(c0, c0 + LANES)
        acc = None
        for j in range(wd):
            a = j % phases
            term = cw[j:j + 1, cols] * win[a, pl.ds(j - a, tl), cols]
            acc = term if acc is None else acc + term
        ybuf[:, cols] = acc + cb[:, cols]
    y = ybuf[...]
    yc_ = y - jnp.mean(y, axis=-1, keepdims=True)
    yn = yc_ * lax.rsqrt(jnp.mean(yc_ * yc_, axis=-1, keepdims=True) + EPS) * nw[...] + nbias[...]
    od_ref[0] = (yn * jax.nn.sigmoid(yn)).astype(od_ref.dtype)

    newc = extc[pl.ds(P + tl - (ws - 1), ws - 1), :]
    newd = extd[pl.ds(P + tl - (wd - 1), wd - 1), :]
    extc[pl.ds(P - (ws - 1), ws - 1), :] = newc
    extd[pl.ds(P - (wd - 1), wd - 1), :] = newd

    @pl.when(t == nt - 1)
    def _():
        scn_ref[0] = newc
        cfn_ref[0] = newd


def conv_branches(proj, sc_buf, cf_buf, sconv_w, cconv_w, cconv_b, cnorm_w, cnorm_b):
    B, L, _ = proj.shape
    ws, BW = sconv_w.shape
    wd = cconv_w.shape[0]
    assert wd - 1 <= CONV_PAD
    tl = _pick(L, (128, 64, 32, 16, 8))
    nt = L // tl

    def slot(s):
        return pl.BlockSpec((1, tl, BW), lambda b, t: (b, t, s))

    full = lambda a: pl.BlockSpec(a.shape, lambda b, t: (0,) * a.ndim)
    cb2, nw2, nb2 = cconv_b.reshape(1, BW), cnorm_w.reshape(1, BW), cnorm_b.reshape(1, BW)
    ospec = pl.BlockSpec((1, tl, BW), lambda b, t: (b, t, 0))
    return pl.pallas_call(
        functools.partial(_conv_kernel, tl=tl, nt=nt, ws=ws, wd=wd),
        grid=(B, nt),
        in_specs=[slot(7), slot(8), slot(9), slot(10), slot(11),
                  pl.BlockSpec((1, ws - 1, BW), lambda b, t: (b, 0, 0)),
                  pl.BlockSpec((1, wd - 1, BW), lambda b, t: (b, 0, 0)),
                  full(sconv_w), full(cconv_w), full(cb2), full(nw2), full(nb2)],
        out_specs=[ospec, ospec,
                   pl.BlockSpec((1, ws - 1, BW), lambda b, t: (b, 0, 0)),
                   pl.BlockSpec((1, wd - 1, BW), lambda b, t: (b, 0, 0))],
        out_shape=[jax.ShapeDtypeStruct((B, L, BW), BF16), jax.ShapeDtypeStruct((B, L, BW), BF16),
                   jax.ShapeDtypeStruct((B, ws - 1, BW), F32), jax.ShapeDtypeStruct((B, wd - 1, BW), F32)],
        scratch_shapes=[pltpu.VMEM((CONV_PAD + tl, BW), F32), pltpu.VMEM((CONV_PAD + tl, BW), F32),
                        pltpu.VMEM((tl, BW), F32),
                        pltpu.VMEM((SUBLANES, tl + CONV_PAD - SUBLANES, BW), F32)],
        compiler_params=_params("parallel", "arbitrary"),
        name="conv_branches",
    )(proj, proj, proj, proj, proj, sc_buf, cf_buf, sconv_w, cconv_w, cb2, nw2, nb2)


def _branches(proj, pos, past, layer, ret_state, sc_buf, cf_buf, lw):
    (ret_gn_w, sconv_w, cconv_w, cconv_b, cnorm_w, cnorm_b, rel_bias) = lw
    B, L, W = proj.shape
    BW = W // N_IN_SLOTS
    H = BW // MOBA_HEAD_DIM
    o_a, ret_new = retention(proj, pos, ret_state, ret_gn_w)
    if past is None:
        o_b = moba_prompt(proj, rel_bias, H)
    else:
        cache_k, cache_v, page_table = past
        o_b = moba_sample(proj, cache_k, cache_v, layer, page_table, rel_bias)
    o_c, o_d, sc_new, cf_new = conv_branches(proj, sc_buf, cf_buf, sconv_w, cconv_w, cconv_b, cnorm_w, cnorm_b)
    k_b = proj[:, :, 5 * BW:6 * BW].reshape(B, L, H, MOBA_HEAD_DIM)
    v_b = proj[:, :, 6 * BW:7 * BW].reshape(B, L, H, MOBA_HEAD_DIM)
    return [a.reshape(B * L, BW) for a in (o_a, o_b, o_c, o_d)], (k_b, v_b, ret_new, sc_new, cf_new)


def kernel(x_prompt, x_sample, c_prompt, c_sample, cache_k, cache_v, state_ret, state_sconv, state_cconv, page_table, w_ada, b_ada, norm_g, w_in, w_gate, b_gate, ret_gn_w, sconv_w, cconv_w, cconv_b, cnorm_w, cnorm_b, w_branch, w_o, w_ffn_in, w_ffn_out, rel_bias):
    B, S, D = x_prompt.shape
    DB, DS, _ = x_sample.shape
    depth = w_in.shape[0]
    BW = D // N_BRANCH
    H = BW // MOBA_HEAD_DIM
    dh_ret = BW // RET_HEADS
    F = w_ffn_out.shape[1]
    page = cache_k.shape[2]
    past_len = page_table.shape[1] * page
    pos_p = jnp.arange(S, dtype=jnp.int32)
    pos_s = past_len + jnp.arange(DS, dtype=jnp.int32)
    Mp, Ms = B * S, DB * DS

    rows = B + DB
    rows_pad = -(-rows // 8) * 8
    c_all = jnp.concatenate([c_prompt, c_sample, jnp.zeros((rows_pad - rows, D), F32)], axis=0)
    mods = ada_mod(c_all, w_ada, b_ada).reshape(depth, rows_pad * N_MOD, 1, D)

    xp, xs = x_prompt, x_sample
    hp = norm_mod(xp, norm_g[0, 0], mods[0], 0, (1, 0))
    hs = norm_mod(xs, norm_g[0, 0], mods[0], B, (1, 0))
    zeros_ret = jnp.zeros((B, RET_HEADS, dh_ret, dh_ret), F32)
    zeros_sc = jnp.zeros((B, sconv_w.shape[1] - 1, BW), F32)
    zeros_cf = jnp.zeros((B, cconv_w.shape[1] - 1, BW), F32)
    outs_p, outs_s = [], []
    for l in range(depth):
        mod = mods[l]
        lw = (ret_gn_w[l], sconv_w[l], cconv_w[l], cconv_b[l], cnorm_w[l], cnorm_b[l], rel_bias)
        nxt = (norm_g[l + 1, 0], mods[l + 1], 1, 0) if l + 1 < depth else None
        hp2, hs2 = hp.reshape(Mp, D), hs.reshape(Ms, D)
        proj_p, proj_s = matmul(hp2, hs2, w_in, l, tm=2048, tn=512)
        br_p, op = _branches(proj_p.reshape(B, S, -1), pos_p, None, l, zeros_ret, zeros_sc, zeros_cf, lw)
        br_s, os_ = _branches(proj_s.reshape(DB, DS, -1), pos_s, (cache_k, cache_v, page_table), l,
                              state_ret[l], state_sconv[l], state_cconv[l], lw)
        mg_p, mg_s = gate_merge(hp2, hs2, br_p, br_s, w_gate, b_gate[l], w_branch, l)
        y_p, y_s = matmul(mg_p, mg_s, w_o, l, tm=2048, tn=512, out_dtype=BF16)
        ffn_mod = (norm_g[l, 2], mod, 4, 3)
        xp, hp = resid_norm(xp, y_p.reshape(B, S, D), norm_g[l, 1], mod, 0, 2, nxt=ffn_mod)
        xs, hs = resid_norm(xs, y_s.reshape(DB, DS, D), norm_g[l, 1], mod, B, 2, nxt=ffn_mod)
        act_p, act_s, w_out_b = ffn_in(hp.reshape(Mp, D), hs.reshape(Ms, D), w_ffn_in, w_ffn_out, l)
        y_p, y_s = matmul(act_p, act_s, w_out_b[None], 0, tm=1024, tn=512, tk=F // 2)
        xp, hp = resid_norm(xp, y_p.reshape(B, S, D), norm_g[l, 3], mod, 0, 5, nxt=nxt)
        xs, hs = resid_norm(xs, y_s.reshape(DB, DS, D), norm_g[l, 3], mod, B, 5, nxt=nxt)
        outs_p.append(op)
        outs_s.append(os_)
    stack = lambda outs, i: jnp.stack([o[i] for o in outs])
    new_k_prompt = stack(outs_p, 0).reshape(depth, B, S // page, page, H, MOBA_HEAD_DIM)
    new_v_prompt = stack(outs_p, 1).reshape(depth, B, S // page, page, H, MOBA_HEAD_DIM)
    return (xp, xs, new_k_prompt, new_v_prompt, stack(outs_s, 0), stack(outs_s, 1),
            stack(outs_p, 2), stack(outs_s, 2), stack(outs_p, 3), stack(outs_s, 3),
            stack(outs_p, 4), stack(outs_s, 4))
```

```python
import functools
import math

import jax
import jax.numpy as jnp
from jax import lax
from jax.experimental import pallas as pl
from jax.experimental.pallas import tpu as pltpu

F32 = jnp.float32
BF16 = jnp.bfloat16

N_BRANCH = 4
N_IN_SLOTS = 12
RET_HEADS = 4
RET_CHUNK = 128
ROPE_BASE = 10000.0
MOBA_HEAD_DIM = 128
MOBA_BLOCK = 256
MOBA_TOPK = 3
REL_BUCKETS = 32
REL_MAX_DIST = 128
N_MOD = 6
EPS = 1e-6
GN_EPS = 1e-5
CONV_PAD = 32
LANES = 128
SUBLANES = 8
MXU_COLS = 256
KV_PAGES_PER_STEP = 16

VMEM_LIMIT_BYTES = 56 * 1024 * 1024


def _params(*sem):
    return pltpu.CompilerParams(dimension_semantics=sem, vmem_limit_bytes=VMEM_LIMIT_BYTES)


def _pick(dim, prefs):
    for p in prefs:
        if dim % p == 0:
            return p
    return dim


def _bdot(a, b):
    return jnp.dot(a.astype(BF16), b.astype(BF16), preferred_element_type=F32)


def _bdot_nt(a, b):
    return lax.dot_general(a.astype(BF16), b.astype(BF16), (((1,), (1,)), ((), ())), preferred_element_type=F32)


def _bdot_tn(a, b):
    return lax.dot_general(a.astype(BF16), b.astype(BF16), (((0,), (0,)), ((), ())), preferred_element_type=F32)


def _fdot_nt(a, b):
    return lax.dot_general(a, b, (((1,), (1,)), ((), ())), precision=lax.Precision.HIGHEST, preferred_element_type=F32)


def _resident(shape, imap):
    return pl.BlockSpec(shape, imap, pipeline_mode=pl.Buffered(1))


def _mm_kernel(xp_ref, xs_ref, w_ref, op_ref, os_ref, *acc, nk):
    i = pl.program_id(0)
    if nk == 1:
        for c0 in range(0, w_ref.shape[1], MXU_COLS):
            cols = slice(c0, c0 + MXU_COLS)
            w = w_ref[:, cols].astype(BF16)
            op_ref[:, cols] = jnp.dot(xp_ref[...], w, preferred_element_type=F32).astype(op_ref.dtype)

        @pl.when(i == 0)
        def _():
            w = w_ref[...].astype(BF16)
            os_ref[0] = jnp.dot(xs_ref[...], w, preferred_element_type=F32).astype(os_ref.dtype)
    else:
        accp, accs = acc
        w = w_ref[...].astype(BF16)
        k = pl.program_id(2)

        def accumulate(x_ref, acc_ref, o_ref, store):
            part = jnp.dot(x_ref[...], w, preferred_element_type=F32)

            @pl.when(k == 0)
            def _():
                acc_ref[...] = part

            @pl.when((k > 0) & (k < nk - 1))
            def _():
                acc_ref[...] += part

            @pl.when(k == nk - 1)
            def _():
                store((acc_ref[...] + part).astype(o_ref.dtype))

        def store_p(v):
            op_ref[...] = v

        def store_s(v):
            os_ref[0] = v

        accumulate(xp_ref, accp, op_ref, store_p)

        @pl.when(i == 0)
        def _():
            accumulate(xs_ref, accs, os_ref, store_s)


def _unblock(o_s):
    nj, Ms, tn = o_s.shape
    return o_s.transpose(1, 0, 2).reshape(Ms, nj * tn)


def matmul(xp, xs, w, layer, *, tm, tn, tk=None, out_dtype=F32):
    Mp, K = xp.shape
    Ms = xs.shape[0]
    N = w.shape[2]
    tm = min(tm, Mp)
    tk = tk or K
    nk = K // tk
    nj = N // tn
    assert Mp % tm == 0 and N % tn == 0 and K % tk == 0 and tn % MXU_COLS == 0
    xp_spec = (_resident((tm, tk), lambda i, j, k: (i, k)) if nk == 1
               else pl.BlockSpec((tm, tk), lambda i, j, k: (i, k)))
    scratch = [] if nk == 1 else [pltpu.VMEM((tm, tn), F32), pltpu.VMEM((Ms, tn), F32)]
    o_p, o_s = pl.pallas_call(
        functools.partial(_mm_kernel, nk=nk),
        grid=(Mp // tm, nj, nk),
        in_specs=[xp_spec,
                  pl.BlockSpec((Ms, tk), lambda i, j, k: (0, k)),
                  pl.BlockSpec((None, tk, tn), lambda i, j, k: (layer, k, j))],
        out_specs=[pl.BlockSpec((tm, tn), lambda i, j, k: (i, j)),
                   pl.BlockSpec((1, Ms, tn), lambda i, j, k: (jnp.where(i == 0, j, nj - 1), 0, 0))],
        out_shape=[jax.ShapeDtypeStruct((Mp, N), out_dtype), jax.ShapeDtypeStruct((nj, Ms, tn), out_dtype)],
        scratch_shapes=scratch,
        compiler_params=_params("arbitrary", "arbitrary", "arbitrary"),
        name="matmul",
    )(xp, xs, w)
    return o_p, _unblock(o_s)


def _gate_merge_kernel(hp_ref, hs_ref, *refs):
    brp = refs[0:4]
    brs = refs[4:8]
    wg_ref, wb_ref, bg_ref, op_ref, os_ref, accp, accs = refs[8:]
    i = pl.program_id(0)
    n = pl.program_id(2)

    def term(h_ref, br_ref, cols):
        wg = wg_ref[:, cols].astype(BF16)
        wb = wb_ref[:, cols].astype(BF16)
        gate = jax.nn.sigmoid(jnp.dot(h_ref[...], wg, preferred_element_type=F32) + bg_ref[0][:, cols])
        return gate * jnp.dot(br_ref[...], wb, preferred_element_type=F32)

    for b in range(N_BRANCH):
        @pl.when(n == b)
        def _(b=b):
            for c0 in range(0, op_ref.shape[1], MXU_COLS):
                cols = slice(c0, c0 + MXU_COLS)
                t = term(hp_ref, brp[b], cols)
                if b == 0:
                    accp[:, cols] = t
                elif b < N_BRANCH - 1:
                    accp[:, cols] += t
                else:
                    op_ref[:, cols] = (accp[:, cols] + t).astype(op_ref.dtype)

        @pl.when((n == b) & (i == 0))
        def _(b=b):
            t = term(hs_ref, brs[b], slice(None))
            if b == 0:
                accs[...] = t
            elif b < N_BRANCH - 1:
                accs[...] += t
            else:
                os_ref[0] = (accs[...] + t).astype(os_ref.dtype)


def gate_merge(hp, hs, branches_p, branches_s, w_gate, b_gate, w_branch, layer, *, tm=2048, tn=256):
    Mp, D = hp.shape
    Ms = hs.shape[0]
    BW = branches_p[0].shape[1]
    tm = min(tm, Mp)
    nj = D // tn
    assert Mp % tm == 0 and D % tn == 0
    bg3 = b_gate.reshape(N_BRANCH, 1, D)
    in_specs = [_resident((tm, D), lambda i, j, n: (i, 0)),
                pl.BlockSpec((Ms, D), lambda i, j, n: (0, 0))]
    in_specs += [_resident((tm, BW), lambda i, j, n: (i, 0)) for _ in range(N_BRANCH)]
    in_specs += [pl.BlockSpec((Ms, BW), lambda i, j, n: (0, 0)) for _ in range(N_BRANCH)]
    in_specs += [pl.BlockSpec((None, D, tn), lambda i, j, n: (layer, 0, n * nj + j)),
                 pl.BlockSpec((None, None, BW, tn), lambda i, j, n: (layer, n, 0, j)),
                 pl.BlockSpec((1, 1, tn), lambda i, j, n: (n, 0, j))]
    o_p, o_s = pl.pallas_call(
        _gate_merge_kernel,
        grid=(Mp // tm, nj, N_BRANCH),
        in_specs=in_specs,
        out_specs=[pl.BlockSpec((tm, tn), lambda i, j, n: (i, j)),
                   pl.BlockSpec((1, Ms, tn), lambda i, j, n: (jnp.where(i == 0, j, nj - 1), 0, 0))],
        out_shape=[jax.ShapeDtypeStruct((Mp, D), BF16), jax.ShapeDtypeStruct((nj, Ms, tn), BF16)],
        scratch_shapes=[pltpu.VMEM((tm, tn), F32), pltpu.VMEM((Ms, tn), F32)],
        compiler_params=_params("arbitrary", "arbitrary", "arbitrary"),
        name="gate_merge",
    )(hp, hs, *branches_p, *branches_s, w_gate, w_branch, bg3)
    return o_p, _unblock(o_s)


def _ffn_in_kernel(hp_ref, hs_ref, wa_ref, wb_ref, wo_ref, op_ref, os_ref, wob_ref):
    wob_ref[...] = wo_ref[...].astype(BF16)
    wa = wa_ref[...].astype(BF16)
    wb = wb_ref[...].astype(BF16)

    def act(h_ref):
        a = jnp.dot(h_ref[...], wa, preferred_element_type=F32)
        b = jnp.dot(h_ref[...], wb, preferred_element_type=F32)
        return a * jax.nn.sigmoid(a) * b

    op_ref[...] = act(hp_ref).astype(op_ref.dtype)

    @pl.when(pl.program_id(0) == 0)
    def _():
        os_ref[0] = act(hs_ref).astype(os_ref.dtype)


def ffn_in(hp, hs, w, w_out, layer, *, tm=2048, tn=256):
    Mp, D = hp.shape
    Ms = hs.shape[0]
    F = w.shape[2] // 2
    Dout = w_out.shape[2]
    tm = min(tm, Mp)
    ni, nj = Mp // tm, F // tn
    slab = F // (ni * nj)
    assert Mp % tm == 0 and F % tn == 0 and F % (ni * nj) == 0 and slab % 16 == 0
    o_p, o_s, w_out_b = pl.pallas_call(
        _ffn_in_kernel,
        grid=(ni, nj),
        in_specs=[_resident((tm, D), lambda i, j: (i, 0)),
                  pl.BlockSpec((Ms, D), lambda i, j: (0, 0)),
                  pl.BlockSpec((None, D, tn), lambda i, j: (layer, 0, j)),
                  pl.BlockSpec((None, D, tn), lambda i, j: (layer, 0, nj + j)),
                  pl.BlockSpec((None, slab, Dout), lambda i, j: (layer, i * nj + j, 0))],
        out_specs=[pl.BlockSpec((tm, tn), lambda i, j: (i, j)),
                   pl.BlockSpec((1, Ms, tn), lambda i, j: (jnp.where(i == 0, j, nj - 1), 0, 0)),
                   pl.BlockSpec((slab, Dout), lambda i, j: (i * nj + j, 0))],
        out_shape=[jax.ShapeDtypeStruct((Mp, F), BF16), jax.ShapeDtypeStruct((nj, Ms, tn), BF16),
                   jax.ShapeDtypeStruct((F, Dout), BF16)],
        compiler_params=_params("arbitrary", "arbitrary"),
        name="ffn_in",
    )(hp, hs, w, w, w_out)
    return o_p, _unblock(o_s), w_out_b


def _ada_kernel(c_ref, w_ref, b_ref, o_ref):
    c = c_ref[...]
    x = c * jax.nn.sigmoid(c)
    o_ref[0] = _bdot(x, w_ref[0]) + b_ref[0]


def ada_mod(c_all, w_ada, b_ada):
    depth, D, N = w_ada.shape
    R = c_all.shape[0]
    tn = 512
    return pl.pallas_call(
        _ada_kernel,
        grid=(depth, N // tn),
        in_specs=[pl.BlockSpec((R, D), lambda l, j: (0, 0)),
                  pl.BlockSpec((1, D, tn), lambda l, j: (l, 0, j)),
                  pl.BlockSpec((1, 1, tn), lambda l, j: (l, 0, j))],
        out_specs=pl.BlockSpec((1, R, tn), lambda l, j: (l, 0, j)),
        out_shape=jax.ShapeDtypeStruct((depth, R, N), F32),
        compiler_params=_params("parallel", "parallel"),
        name="ada_mod",
    )(c_all, w_ada, b_ada.reshape(depth, 1, N))


def _rms(x, g):
    return x * lax.rsqrt(jnp.mean(x * x, axis=-1, keepdims=True) + EPS) * g


def _norm_mod_kernel(x_ref, g_ref, sc_ref, sh_ref, h_ref):
    h = _rms(x_ref[0], g_ref[...]) * (1.0 + sc_ref[0]) + sh_ref[0]
    h_ref[0] = h.astype(h_ref.dtype)


def _mod_spec(D, row0, slot):
    return pl.BlockSpec((1, 1, D), lambda b, t: ((row0 + b) * N_MOD + slot, 0, 0))


def norm_mod(x, g, mod, row0, slots):
    B, L, D = x.shape
    tl = _pick(L, (256, 128, 64, 32, 16, 8))
    return pl.pallas_call(
        _norm_mod_kernel,
        grid=(B, L // tl),
        in_specs=[pl.BlockSpec((1, tl, D), lambda b, t: (b, t, 0)),
                  pl.BlockSpec((1, D), lambda b, t: (0, 0)),
                  _mod_spec(D, row0, slots[0]), _mod_spec(D, row0, slots[1])],
        out_specs=pl.BlockSpec((1, tl, D), lambda b, t: (b, t, 0)),
        out_shape=jax.ShapeDtypeStruct((B, L, D), BF16),
        compiler_params=_params("parallel", "parallel"),
        name="norm_mod",
    )(x, g.reshape(1, D), mod, mod)


def _resid_kernel(x_ref, y_ref, ga_ref, gate_ref, *rest, emit_h):
    xn = x_ref[0] + gate_ref[0] * _rms(y_ref[0].astype(F32), ga_ref[...])
    if emit_h:
        gb_ref, sc_ref, sh_ref, xo_ref, h_ref = rest
        xo_ref[0] = xn
        h_ref[0] = (_rms(xn, gb_ref[...]) * (1.0 + sc_ref[0]) + sh_ref[0]).astype(h_ref.dtype)
    else:
        (xo_ref,) = rest
        xo_ref[0] = xn


def resid_norm(x, y, g_a, mod, row0, gate_slot, nxt=None):
    B, L, D = x.shape
    tl = _pick(L, (256, 128, 64, 32, 16, 8))
    xspec = pl.BlockSpec((1, tl, D), lambda b, t: (b, t, 0))
    gspec = pl.BlockSpec((1, D), lambda b, t: (0, 0))
    in_specs = [xspec, xspec, gspec, _mod_spec(D, row0, gate_slot)]
    args = [x, y, g_a.reshape(1, D), mod]
    out_specs = [xspec]
    out_shape = [jax.ShapeDtypeStruct((B, L, D), F32)]
    if nxt is not None:
        g_b, mod_n, sc_slot, sh_slot = nxt
        in_specs += [gspec, _mod_spec(D, row0, sc_slot), _mod_spec(D, row0, sh_slot)]
        args += [g_b.reshape(1, D), mod_n, mod_n]
        out_specs.append(xspec)
        out_shape.append(jax.ShapeDtypeStruct((B, L, D), BF16))
    res = pl.pallas_call(
        functools.partial(_resid_kernel, emit_h=nxt is not None),
        grid=(B, L // tl),
        in_specs=in_specs, out_specs=out_specs, out_shape=out_shape,
        compiler_params=_params("parallel", "parallel"),
        name="resid_norm",
    )(*args)
    return (res[0], res[1]) if nxt is not None else (res[0], None)


def _ret_kernel(q_ref, k_ref, v_ref, g_ref, cos_ref, sin_ref, dec_ref, qd_ref, kd_ref, gl_ref, s0_ref, gnw_ref,
                o_ref, so_ref, s_scr, *, nc, dh):
    c = pl.program_id(1)

    @pl.when(c == 0)
    def _():
        s_scr[...] = s0_ref[0]

    half = dh // 2
    cos = cos_ref[...]
    sin = sin_ref[...]

    def rot(x):
        x1 = x[:, :half]
        x2 = x[:, half:]
        return jnp.concatenate([x1 * cos - x2 * sin, x1 * sin + x2 * cos], axis=-1)

    for h in range(s_scr.shape[0]):
        cols = slice(h * dh, (h + 1) * dh)
        q = rot(q_ref[0, :, cols])
        k = rot(k_ref[0, :, cols]) * (dh ** -0.5)
        v = v_ref[0, :, cols]
        S = s_scr[h]
        scores = _bdot_nt(q, k) * dec_ref[h]
        o = _bdot(scores, v) + _bdot(q * qd_ref[h], S)
        s_scr[h] = gl_ref[h, 0:1, :] * S + _bdot_tn(k * kd_ref[h], v)
        oc = o - jnp.mean(o, axis=-1, keepdims=True)
        on = oc * lax.rsqrt(jnp.mean(oc * oc, axis=-1, keepdims=True) + GN_EPS)
        g = g_ref[0, :, cols]
        o_ref[0, :, cols] = (on * gnw_ref[:, cols] * (g * jax.nn.sigmoid(g))).astype(o_ref.dtype)

    @pl.when(c == nc - 1)
    def _():
        so_ref[0] = s_scr[...]


def retention(proj, pos, state0, gn_w):
    B, L, _ = proj.shape
    H = RET_HEADS
    BW = gn_w.shape[0]
    dh = BW // H
    half = dh // 2
    C = RET_CHUNK if L % RET_CHUNK == 0 else L
    nc = L // C
    inv = jnp.exp(-math.log(ROPE_BASE) * jnp.arange(half, dtype=F32) / half)
    ang = pos.astype(F32)[:, None] * inv[None, :]
    cos, sin = jnp.cos(ang), jnp.sin(ang)
    log_g = jnp.log1p(-jnp.exp2(-5.0 - jnp.arange(H, dtype=F32)))
    idx = jnp.arange(C, dtype=F32)
    diff = idx[:, None] - idx[None, :]
    causal = diff >= 0
    dec = jnp.where(causal[None], jnp.exp(log_g[:, None, None] * jnp.where(causal, diff, 0.0)[None]), 0.0)
    qd = jnp.broadcast_to(jnp.exp(log_g[:, None] * (idx[None, :] + 1.0))[:, :, None], (H, C, dh))
    kd = jnp.broadcast_to(jnp.exp(log_g[:, None] * ((C - 1.0) - idx[None, :]))[:, :, None], (H, C, dh))
    gl = jnp.broadcast_to(jnp.exp(log_g * C)[:, None, None], (H, 8, dh))

    def slot(s):
        return pl.BlockSpec((1, C, BW), lambda b, c: (b, c, s))

    tab = pl.BlockSpec((C, half), lambda b, c: (c, 0))
    const = lambda a: pl.BlockSpec(a.shape, lambda b, c: (0,) * a.ndim)
    return pl.pallas_call(
        functools.partial(_ret_kernel, nc=nc, dh=dh),
        grid=(B, nc),
        in_specs=[slot(0), slot(1), slot(2), slot(3), tab, tab, const(dec), const(qd), const(kd), const(gl),
                  pl.BlockSpec((1, H, dh, dh), lambda b, c: (b, 0, 0, 0)),
                  pl.BlockSpec((1, BW), lambda b, c: (0, 0))],
        out_specs=[pl.BlockSpec((1, C, BW), lambda b, c: (b, c, 0)),
                   pl.BlockSpec((1, H, dh, dh), lambda b, c: (b, 0, 0, 0))],
        out_shape=[jax.ShapeDtypeStruct((B, L, BW), BF16),
                   jax.ShapeDtypeStruct((B, H, dh, dh), F32)],
        scratch_shapes=[pltpu.VMEM((H, dh, dh), F32)],
        compiler_params=_params("parallel", "arbitrary"),
        name="retention",
    )(proj, proj, proj, proj, cos, sin, dec, qd, kd, gl, state0, gn_w.reshape(1, BW))


def t5_bucket(n):
    max_exact = REL_BUCKETS // 2
    nf = jnp.maximum(n, 1).astype(F32)
    large = max_exact + (jnp.log(nf / max_exact) / math.log(REL_MAX_DIST / max_exact) * (REL_BUCKETS - max_exact)).astype(jnp.int32)
    large = jnp.minimum(large, REL_BUCKETS - 1)
    return jnp.where(n < max_exact, n, large)


def _bias_lookup(dist, rb_rows):
    k = jnp.arange(REL_BUCKETS)
    return jnp.sum(jnp.where(t5_bucket(dist)[..., None] == k, rb_rows, 0.0), axis=-1)


def _select_blocks(gate, n_elig, nb):
    col = lax.broadcasted_iota(jnp.int32, gate.shape, 1)
    rank = jnp.zeros(gate.shape, jnp.int32)
    for m in range(nb):
        gm = gate[:, m:m + 1]
        beats = (gm > gate) | ((gm == gate) & (m < col))
        rank = rank + jnp.where(beats, jnp.where(m < n_elig, 1, 0), 0)
    return (col < n_elig) & (rank < MOBA_TOPK)


def _moba_prompt_kernel(q_ref, k_ref, v_ref, tb_ref, o_ref, kb_scr, vb_scr, *, nb, blk, scale):
    L = nb * blk
    kb_scr[...] = k_ref[0].astype(BF16)
    vb_scr[...] = v_ref[0].astype(BF16)
    kbar = jnp.concatenate(
        [jnp.mean(k_ref[0, j * blk:(j + 1) * blk, :], axis=0, keepdims=True) for j in range(nb)], axis=0)
    q = q_ref[0]
    gate = _fdot_nt(kbar, q)
    n_idx = lax.broadcasted_iota(jnp.int32, (nb, L), 0)
    own = lax.broadcasted_iota(jnp.int32, (nb, L), 1) // blk
    rank = jnp.zeros((nb, L), jnp.int32)
    for m in range(nb):
        gm = gate[m:m + 1, :]
        beats = (gm > gate) | ((gm == gate) & (m < n_idx))
        rank = rank + jnp.where(beats & (m < own), 1, 0)
    sel = jnp.where((n_idx < own) & (rank < MOBA_TOPK), 1.0, 0.0).T

    r = lax.broadcasted_iota(jnp.int32, (blk, blk), 0)
    c = lax.broadcasted_iota(jnp.int32, (blk, blk), 1)
    causal = c <= r
    far_bias = tb_ref[0, 0:1, 0:1]
    for i in range(nb):
        rows = slice(i * blk, (i + 1) * blk)
        s = _bdot_nt(q[rows] * scale, kb_scr[0:(i + 1) * blk, :])
        parts = []
        for j in range(i + 1):
            sj = s[:, j * blk:(j + 1) * blk]
            if j == i:
                parts.append(jnp.where(causal, sj + tb_ref[0, :, blk:], -jnp.inf))
            else:
                bias = tb_ref[0, :, :blk] if j == i - 1 else far_bias
                parts.append(jnp.where(sel[rows, j:j + 1] > 0.0, sj + bias, -jnp.inf))
        s = jnp.concatenate(parts, axis=1) if len(parts) > 1 else parts[0]
        m = jnp.max(s, axis=-1, keepdims=True)
        p = jnp.exp(s - m)
        l = jnp.sum(p, axis=-1, keepdims=True)
        o_ref[0, rows, :] = (_bdot(p, vb_scr[0:(i + 1) * blk, :]) / l).astype(o_ref.dtype)


def moba_prompt(proj, rel_bias, H):
    B, L, _ = proj.shape
    blk = MOBA_BLOCK
    dh = MOBA_HEAD_DIM
    nb = L // blk
    assert L % blk == 0 and blk > REL_MAX_DIST
    P = 2 * blk + 1
    mm = jnp.arange(P)
    dvec = jnp.where(mm <= blk, blk - mm, 2 * blk - 1)
    bvec = _bias_lookup(dvec[None, :], rel_bias.astype(F32).T[:, None, :])
    tb = jnp.tile(bvec, (1, blk))[:, :blk * (P - 1)].reshape(H, blk, P - 1)
    return pl.pallas_call(
        functools.partial(_moba_prompt_kernel, nb=nb, blk=blk, scale=dh ** -0.5),
        grid=(B, H),
        in_specs=[pl.BlockSpec((1, L, dh), lambda b, h: (b, 0, 4 * H + h)),
                  pl.BlockSpec((1, L, dh), lambda b, h: (b, 0, 5 * H + h)),
                  pl.BlockSpec((1, L, dh), lambda b, h: (b, 0, 6 * H + h)),
                  pl.BlockSpec((1, blk, 2 * blk), lambda b, h: (h, 0, 0))],
        out_specs=pl.BlockSpec((1, L, dh), lambda b, h: (b, 0, h)),
        out_shape=jax.ShapeDtypeStruct((B, L, H * dh), BF16),
        scratch_shapes=[pltpu.VMEM((L, dh), BF16), pltpu.VMEM((L, dh), BF16)],
        compiler_params=_params("parallel", "parallel"),
        name="moba_prompt",
    )(proj, proj, proj, tb)


def _moba_sample_kernel(pt_ref, q_ref, kn_ref, vn_ref, bown_ref, blast_ref, bfar_ref, *refs,
                        pps, ppb, ns, nbp, H, scale, inv_blk):
    k_refs = refs[:pps]
    v_refs = refs[pps:2 * pps]
    o_ref = refs[2 * pps]
    q_scr, p_scr, gate_scr, ksum_scr, l_scr, acc_scr = refs[2 * pps + 1:]
    s = pl.program_id(1)
    ds = q_ref.shape[1]
    dh = q_scr.shape[1]
    R = H * ds
    page = k_refs[0].shape[2] // H
    n_pages = ns * pps

    def head_major(ref):
        return jnp.concatenate([ref[0, :, h * dh:(h + 1) * dh] for h in range(H)], axis=0)

    @pl.when(s == 0)
    def _():
        q_scr[...] = head_major(q_ref)
        gate_scr[...] = jnp.zeros_like(gate_scr)

    @pl.when(s < ns)
    def _():
        lane = lax.broadcasted_iota(jnp.int32, (ds, LANES), 1)
        for i in range(pps):
            pg = s * pps + i
            for h in range(H):
                rows = slice(h * ds, (h + 1) * ds)
                kh = k_refs[i][0, 0, pl.ds(h, page, stride=H), :]
                p_scr[pg, rows, :] = _bdot_nt(q_scr[rows, :], kh) * scale
                tot = jnp.sum(kh, axis=0, keepdims=True)
                if i % ppb == 0:
                    ksum_scr[h:h + 1, :] = tot
                else:
                    ksum_scr[h:h + 1, :] += tot
                if i % ppb == ppb - 1:
                    g = jnp.sum(q_scr[rows, :] * (ksum_scr[h:h + 1, :] * inv_blk), axis=-1, keepdims=True)
                    gate_scr[rows, :] += jnp.where(lane == pg // ppb, g, 0.0)

    @pl.when(s == ns)
    def _():
        sel = jnp.where(_select_blocks(gate_scr[...], nbp, nbp), 1.0, 0.0)
        far = bfar_ref[:, 0:1]
        q = q_scr[...]
        kn = head_major(kn_ref)
        vn = head_major(vn_ref)
        so = _bdot_nt(q, kn) * scale + bown_ref[...]
        r = lax.broadcasted_iota(jnp.int32, (R, R), 0)
        c = lax.broadcasted_iota(jnp.int32, (R, R), 1)
        so = jnp.where((c // ds == r // ds) & (c % ds <= r % ds), so, -jnp.inf)
        m_own = jnp.max(so, axis=-1, keepdims=True)

        picked = [jnp.broadcast_to(sel[:, n:n + 1] > 0.0, (R, page)) for n in range(nbp)]

        def masked(pg):
            return jnp.where(picked[pg // ppb], p_scr[pg], -jnp.inf)

        mx = masked(0)
        for pg in range(1, n_pages - 1):
            mx = jnp.maximum(mx, masked(pg))
        last = masked(n_pages - 1) + blast_ref[...]
        m = jnp.maximum(jnp.max(jnp.maximum(mx + far, last), axis=-1, keepdims=True), m_own)
        shift = far - m
        lsum = jnp.exp(last - m)
        p_scr[n_pages - 1] = lsum
        for pg in range(n_pages - 1):
            p = jnp.exp(masked(pg) + shift)
            p_scr[pg] = p
            lsum = lsum + p
        p_own = jnp.exp(so - m)
        l_scr[...] = jnp.sum(lsum, axis=-1, keepdims=True) + jnp.sum(p_own, axis=-1, keepdims=True)
        acc_scr[...] = _bdot(p_own, vn)

    @pl.when(s >= ns)
    def _():
        for i in range(pps):
            pg = (s - ns) * pps + i
            for h in range(H):
                rows = slice(h * ds, (h + 1) * ds)
                vh = v_refs[i][0, 0, pl.ds(h, page, stride=H), :]
                acc_scr[rows, :] += _bdot(p_scr[pg, rows, :], vh)

        @pl.when(s == 2 * ns - 1)
        def _():
            o = acc_scr[...] / l_scr[...]
            for h in range(H):
                o_ref[0, :, h * dh:(h + 1) * dh] = o[h * ds:(h + 1) * ds, :].astype(o_ref.dtype)


def moba_sample(proj, cache_k, cache_v, layer, page_table, rel_bias):
    DB, DS, _ = proj.shape
    _, n_pool, page, H, dh = cache_k.shape
    n_pages = page_table.shape[1]
    past_len = n_pages * page
    ppb = MOBA_BLOCK // page
    pps = min(KV_PAGES_PER_STEP, n_pages)
    nbp = past_len // MOBA_BLOCK
    assert past_len % MOBA_BLOCK == 0 and DS <= MOBA_BLOCK and page >= REL_MAX_DIST and DS <= page
    assert MOBA_BLOCK % page == 0 and n_pages % pps == 0 and pps % ppb == 0 and nbp <= LANES and page == LANES
    ns = n_pages // pps
    R = H * DS
    BW = H * dh
    rb_rows = jnp.repeat(rel_bias.astype(F32).T, DS, axis=0)[:, None, :]
    hq_q = jnp.tile(jnp.arange(DS), H)
    b_own = _bias_lookup(jnp.maximum(hq_q[:, None] - hq_q[None, :], 0), rb_rows)
    b_last = _bias_lookup(page + hq_q[:, None] - jnp.arange(page)[None, :], rb_rows)
    b_far = jnp.broadcast_to(rb_rows[:, :, REL_BUCKETS - 1], (R, LANES))
    ck = cache_k.reshape(cache_k.shape[0], n_pool, page * H, dh)
    cv = cache_v.reshape(cache_v.shape[0], n_pool, page * H, dh)

    def kspec(i):
        return pl.BlockSpec((1, 1, page * H, dh),
                            lambda b, s, pt: (layer, pt[b, jnp.minimum(s, ns - 1) * pps + i], 0, 0))

    def vspec(i):
        return pl.BlockSpec((1, 1, page * H, dh),
                            lambda b, s, pt: (layer, pt[b, jnp.maximum(s - ns, 0) * pps + i], 0, 0))

    def slot(sl):
        return pl.BlockSpec((1, DS, BW), lambda b, s, pt: (b, 0, sl))

    const = lambda a: pl.BlockSpec(a.shape, lambda b, s, pt: (0,) * a.ndim)
    return pl.pallas_call(
        functools.partial(_moba_sample_kernel, pps=pps, ppb=ppb, ns=ns, nbp=nbp, H=H, scale=dh ** -0.5,
                          inv_blk=1.0 / MOBA_BLOCK),
        grid_spec=pltpu.PrefetchScalarGridSpec(
            num_scalar_prefetch=1,
            grid=(DB, 2 * ns),
            in_specs=[slot(4), slot(5), slot(6), const(b_own), const(b_last), const(b_far)]
                     + [kspec(i) for i in range(pps)] + [vspec(i) for i in range(pps)],
            out_specs=pl.BlockSpec((1, DS, BW), lambda b, s, pt: (b, 0, 0)),
            scratch_shapes=[pltpu.VMEM((R, dh), F32), pltpu.VMEM((n_pages, R, page), F32),
                            pltpu.VMEM((R, LANES), F32), pltpu.VMEM((H, dh), F32),
                            pltpu.VMEM((R, 1), F32), pltpu.VMEM((R, dh), F32)]),
        out_shape=jax.ShapeDtypeStruct((DB, DS, BW), BF16),
        compiler_params=_params("parallel", "arbitrary"),
        name="moba_sample",
    )(page_table, proj, proj, proj, b_own, b_last, b_far, *([ck] * pps), *([cv] * pps))


def _conv_kernel(p7, p8, p9, p10, p11, scb, cfb, sw, cw, cb, nw, nbias, oc_ref, od_ref, scn_ref, cfn_ref,
                 extc, extd, ybuf, win, *, tl, nt, ws, wd):
    t = pl.program_id(1)
    P = CONV_PAD
    W = ybuf.shape[1]

    @pl.when(t == 0)
    def _():
        extc[pl.ds(P - (ws - 1), ws - 1), :] = scb[0]
        extd[pl.ds(P - (wd - 1), wd - 1), :] = cfb[0]

    extc[pl.ds(P, tl), :] = p8[0] * p9[0]
    extd[pl.ds(P, tl), :] = p10[0] * jax.nn.sigmoid(p11[0])

    yc = sw[0:1, :] * extc[pl.ds(P - (ws - 1), tl), :]
    for j in range(1, ws):
        yc = yc + sw[j:j + 1, :] * extc[pl.ds(P - (ws - 1) + j, tl), :]
    oc_ref[0] = (p7[0] * yc).astype(oc_ref.dtype)

    phases = min(SUBLANES, wd)
    for a in range(phases):
        rows = tl + (wd - 1 - a) // SUBLANES * SUBLANES
        win[a, pl.ds(0, rows), :] = extd[pl.ds(P - (wd - 1) + a, rows), :]
    for c0 in range(0, W, LANES):
        cols = slice(c0, c0 + LANES)
        acc = None
        for j in range(wd):
            a = j % phases
            term = cw[j:j + 1, cols] * win[a, pl.ds(j - a, tl), cols]
            acc = term if acc is None else acc + term
        ybuf[:, cols] = acc + cb[:, cols]
    y = ybuf[...]
    yc_ = y - jnp.mean(y, axis=-1, keepdims=True)
    yn = yc_ * lax.rsqrt(jnp.mean(yc_ * yc_, axis=-1, keepdims=True) + EPS) * nw[...] + nbias[...]
    od_ref[0] = (yn * jax.nn.sigmoid(yn)).astype(od_ref.dtype)

    newc = extc[pl.ds(P + tl - (ws - 1), ws - 1), :]
    newd = extd[pl.ds(P + tl - (wd - 1), wd - 1), :]
    extc[pl.ds(P - (ws - 1), ws - 1), :] = newc
    extd[pl.ds(P - (wd - 1), wd - 1), :] = newd

    @pl.when(t == nt - 1)
    def _():
        scn_ref[0] = newc
        cfn_ref[0] = newd


def conv_branches(proj, sc_buf, cf_buf, sconv_w, cconv_w, cconv_b, cnorm_w, cnorm_b):
    B, L, _ = proj.shape
    ws, BW = sconv_w.shape
    wd = cconv_w.shape[0]
    assert wd - 1 <= CONV_PAD
    tl = _pick(L, (128, 64, 32, 16, 8))
    nt = L // tl

    def slot(s):
        return pl.BlockSpec((1, tl, BW), lambda b, t: (b, t, s))

    full = lambda a: pl.BlockSpec(a.shape, lambda b, t: (0,) * a.ndim)
    cb2, nw2, nb2 = cconv_b.reshape(1, BW), cnorm_w.reshape(1, BW), cnorm_b.reshape(1, BW)
    ospec = pl.BlockSpec((1, tl, BW), lambda b, t: (b, t, 0))
    return pl.pallas_call(
        functools.partial(_conv_kernel, tl=tl, nt=nt, ws=ws, wd=wd),
        grid=(B, nt),
        in_specs=[slot(7), slot(8), slot(9), slot(10), slot(11),
                  pl.BlockSpec((1, ws - 1, BW), lambda b, t: (b, 0, 0)),
                  pl.BlockSpec((1, wd - 1, BW), lambda b, t: (b, 0, 0)),
                  full(sconv_w), full(cconv_w), full(cb2), full(nw2), full(nb2)],
        out_specs=[ospec, ospec,
                   pl.BlockSpec((1, ws - 1, BW), lambda b, t: (b, 0, 0)),
                   pl.BlockSpec((1, wd - 1, BW), lambda b, t: (b, 0, 0))],
        out_shape=[jax.ShapeDtypeStruct((B, L, BW), BF16), jax.ShapeDtypeStruct((B, L, BW), BF16),
                   jax.ShapeDtypeStruct((B, ws - 1, BW), F32), jax.ShapeDtypeStruct((B, wd - 1, BW), F32)],
        scratch_shapes=[pltpu.VMEM((CONV_PAD + tl, BW), F32), pltpu.VMEM((CONV_PAD + tl, BW), F32),
                        pltpu.VMEM((tl, BW), F32),
                        pltpu.VMEM((SUBLANES, tl + CONV_PAD - SUBLANES, BW), F32)],
        compiler_params=_params("parallel", "arbitrary"),
        name="conv_branches",
    )(proj, proj, proj, proj, proj, sc_buf, cf_buf, sconv_w, cconv_w, cb2, nw2, nb2)


def _branches(proj, pos, past, layer, ret_state, sc_buf, cf_buf, lw):
    (ret_gn_w, sconv_w, cconv_w, cconv_b, cnorm_w, cnorm_b, rel_bias) = lw
    B, L, W = proj.shape
    BW = W // N_IN_SLOTS
    H = BW // MOBA_HEAD_DIM
    o_a, ret_new = retention(proj, pos, ret_state, ret_gn_w)
    if past is None:
        o_b = moba_prompt(proj, rel_bias, H)
    else:
        cache_k, cache_v, page_table = past
        o_b = moba_sample(proj, cache_k, cache_v, layer, page_table, rel_bias)
    o_c, o_d, sc_new, cf_new = conv_branches(proj, sc_buf, cf_buf, sconv_w, cconv_w, cconv_b, cnorm_w, cnorm_b)
    k_b = proj[:, :, 5 * BW:6 * BW].reshape(B, L, H, MOBA_HEAD_DIM)
    v_b = proj[:, :, 6 * BW:7 * BW].reshape(B, L, H, MOBA_HEAD_DIM)
    return [a.reshape(B * L, BW) for a in (o_a, o_b, o_c, o_d)], (k_b, v_b, ret_new, sc_new, cf_new)


def kernel(x_prompt, x_sample, c_prompt, c_sample, cache_k, cache_v, state_ret, state_sconv, state_cconv, page_table, w_ada, b_ada, norm_g, w_in, w_gate, b_gate, ret_gn_w, sconv_w, cconv_w, cconv_b, cnorm_w, cnorm_b, w_branch, w_o, w_ffn_in, w_ffn_out, rel_bias):
    B, S, D = x_prompt.shape
    DB, DS, _ = x_sample.shape
    depth = w_in.shape[0]
    BW = D // N_BRANCH
    H = BW // MOBA_HEAD_DIM
    dh_ret = BW // RET_HEADS
    F = w_ffn_out.shape[1]
    page = cache_k.shape[2]
    past_len = page_table.shape[1] * page
    pos_p = jnp.arange(S, dtype=jnp.int32)
    pos_s = past_len + jnp.arange(DS, dtype=jnp.int32)
    Mp, Ms = B * S, DB * DS

    rows = B + DB
    rows_pad = -(-rows // 8) * 8
    c_all = jnp.concatenate([c_prompt, c_sample, jnp.zeros((rows_pad - rows, D), F32)], axis=0)
    mods = ada_mod(c_all, w_ada, b_ada).reshape(depth, rows_pad * N_MOD, 1, D)

    xp, xs = x_prompt, x_sample
    hp = norm_mod(xp, norm_g[0, 0], mods[0], 0, (1, 0))
    hs = norm_mod(xs, norm_g[0, 0], mods[0], B, (1, 0))
    zeros_ret = jnp.zeros((B, RET_HEADS, dh_ret, dh_ret), F32)
    zeros_sc = jnp.zeros((B, sconv_w.shape[1] - 1, BW), F32)
    zeros_cf = jnp.zeros((B, cconv_w.shape[1] - 1, BW), F32)
    outs_p, outs_s = [], []
    for l in range(depth):
        mod = mods[l]
        lw = (ret_gn_w[l], sconv_w[l], cconv_w[l], cconv_b[l], cnorm_w[l], cnorm_b[l], rel_bias)
        nxt = (norm_g[l + 1, 0], mods[l + 1], 1, 0) if l + 1 < depth else None
        hp2, hs2 = hp.reshape(Mp, D), hs.reshape(Ms, D)
        proj_p, proj_s = matmul(hp2, hs2, w_in, l, tm=2048, tn=512)
        br_p, op = _branches(proj_p.reshape(B, S, -1), pos_p, None, l, zeros_ret, zeros_sc, zeros_cf, lw)
        br_s, os_ = _branches(proj_s.reshape(DB, DS, -1), pos_s, (cache_k, cache_v, page_table), l,
                              state_ret[l], state_sconv[l], state_cconv[l], lw)
        mg_p, mg_s = gate_merge(hp2, hs2, br_p, br_s, w_gate, b_gate[l], w_branch, l)
        y_p, y_s = matmul(mg_p, mg_s, w_o, l, tm=2048, tn=512, out_dtype=BF16)
        ffn_mod = (norm_g[l, 2], mod, 4, 3)
        xp, hp = resid_norm(xp, y_p.reshape(B, S, D), norm_g[l, 1], mod, 0, 2, nxt=ffn_mod)
        xs, hs = resid_norm(xs, y_s.reshape(DB, DS, D), norm_g[l, 1], mod, B, 2, nxt=ffn_mod)
        act_p, act_s, w_out_b = ffn_in(hp.reshape(Mp, D), hs.reshape(Ms, D), w_ffn_in, w_ffn_out, l)
        y_p, y_s = matmul(act_p, act_s, w_out_b[None], 0, tm=1024, tn=512, tk=F // 2, out_dtype=BF16)
        xp, hp = resid_norm(xp, y_p.reshape(B, S, D), norm_g[l, 3], mod, 0, 5, nxt=nxt)
        xs, hs = resid_norm(xs, y_s.reshape(DB, DS, D), norm_g[l, 3], mod, B, 5, nxt=nxt)
        outs_p.append(op)
        outs_s.append(os_)
    stack = lambda outs, i: jnp.stack([o[i] for o in outs])
    new_k_prompt = stack(outs_p, 0).reshape(depth, B, S // page, page, H, MOBA_HEAD_DIM)
    new_v_prompt = stack(outs_p, 1).reshape(depth, B, S // page, page, H, MOBA_HEAD_DIM)
    return (xp, xs, new_k_prompt, new_v_prompt, stack(outs_s, 0), stack(outs_s, 1),
            stack(outs_p, 2), stack(outs_s, 2), stack(outs_p, 3), stack(outs_s, 3),
            stack(outs_p, 4), stack(outs_s, 4))
```

```python
import functools
import math

import jax
import jax.numpy as jnp
from jax import lax
from jax.experimental import pallas as pl
from jax.experimental.pallas import tpu as pltpu

F32 = jnp.float32
BF16 = jnp.bfloat16

N_BRANCH = 4
N_IN_SLOTS = 12
RET_HEADS = 4
RET_CHUNK = 128
ROPE_BASE = 10000.0
MOBA_HEAD_DIM = 128
MOBA_BLOCK = 256
MOBA_TOPK = 3
REL_BUCKETS = 32
REL_MAX_DIST = 128
N_MOD = 6
EPS = 1e-6
GN_EPS = 1e-5
CONV_PAD = 32
LANES = 128
SUBLANES = 8
MXU_COLS = 256
KV_PAGES_PER_STEP = 16

VMEM_LIMIT_BYTES = 56 * 1024 * 1024


def _params(*sem):
    return pltpu.CompilerParams(dimension_semantics=sem, vmem_limit_bytes=VMEM_LIMIT_BYTES)


def _pick(dim, prefs):
    for p in prefs:
        if dim % p == 0:
            return p
    return dim


def _bdot(a, b):
    return jnp.dot(a.astype(BF16), b.astype(BF16), preferred_element_type=F32)


def _bdot_nt(a, b):
    return lax.dot_general(a.astype(BF16), b.astype(BF16), (((1,), (1,)), ((), ())), preferred_element_type=F32)


def _bdot_tn(a, b):
    return lax.dot_general(a.astype(BF16), b.astype(BF16), (((0,), (0,)), ((), ())), preferred_element_type=F32)


def _fdot_nt(a, b):
    return lax.dot_general(a, b, (((1,), (1,)), ((), ())), precision=lax.Precision.HIGHEST, preferred_element_type=F32)


def _resident(shape, imap):
    return pl.BlockSpec(shape, imap, pipeline_mode=pl.Buffered(1))


def _mm_kernel(xp_ref, xs_ref, w_ref, op_ref, os_ref, *acc, nk):
    i = pl.program_id(0)
    if nk == 1:
        for c0 in range(0, w_ref.shape[1], MXU_COLS):
            cols = slice(c0, c0 + MXU_COLS)
            w = w_ref[:, cols].astype(BF16)
            op_ref[:, cols] = jnp.dot(xp_ref[...], w, preferred_element_type=F32).astype(op_ref.dtype)

        @pl.when(i == 0)
        def _():
            w = w_ref[...].astype(BF16)
            os_ref[0] = jnp.dot(xs_ref[...], w, preferred_element_type=F32).astype(os_ref.dtype)
    else:
        accp, accs = acc
        w = w_ref[...].astype(BF16)
        k = pl.program_id(2)

        def accumulate(x_ref, acc_ref, o_ref, store):
            part = jnp.dot(x_ref[...], w, preferred_element_type=F32)

            @pl.when(k == 0)
            def _():
                acc_ref[...] = part

            @pl.when((k > 0) & (k < nk - 1))
            def _():
                acc_ref[...] += part

            @pl.when(k == nk - 1)
            def _():
                store((acc_ref[...] + part).astype(o_ref.dtype))

        def store_p(v):
            op_ref[...] = v

        def store_s(v):
            os_ref[0] = v

        accumulate(xp_ref, accp, op_ref, store_p)

        @pl.when(i == 0)
        def _():
            accumulate(xs_ref, accs, os_ref, store_s)


def _unblock(o_s):
    nj, Ms, tn = o_s.shape
    return o_s.transpose(1, 0, 2).reshape(Ms, nj * tn)


def matmul(xp, xs, w, layer, *, tm, tn, tk=None, out_dtype=F32):
    Mp, K = xp.shape
    Ms = xs.shape[0]
    N = w.shape[2]
    tm = min(tm, Mp)
    tk = tk or K
    nk = K // tk
    nj = N // tn
    assert Mp % tm == 0 and N % tn == 0 and K % tk == 0 and tn % MXU_COLS == 0
    xp_spec = (_resident((tm, tk), lambda i, j, k: (i, k)) if nk == 1
               else pl.BlockSpec((tm, tk), lambda i, j, k: (i, k)))
    scratch = [] if nk == 1 else [pltpu.VMEM((tm, tn), F32), pltpu.VMEM((Ms, tn), F32)]
    o_p, o_s = pl.pallas_call(
        functools.partial(_mm_kernel, nk=nk),
        grid=(Mp // tm, nj, nk),
        in_specs=[xp_spec,
                  pl.BlockSpec((Ms, tk), lambda i, j, k: (0, k)),
                  pl.BlockSpec((None, tk, tn), lambda i, j, k: (layer, k, j))],
        out_specs=[pl.BlockSpec((tm, tn), lambda i, j, k: (i, j)),
                   pl.BlockSpec((1, Ms, tn), lambda i, j, k: (jnp.where(i == 0, j, nj - 1), 0, 0))],
        out_shape=[jax.ShapeDtypeStruct((Mp, N), out_dtype), jax.ShapeDtypeStruct((nj, Ms, tn), out_dtype)],
        scratch_shapes=scratch,
        compiler_params=_params("arbitrary", "arbitrary", "arbitrary"),
        name="matmul",
    )(xp, xs, w)
    return o_p, _unblock(o_s)


def _gate_merge_kernel(hp_ref, hs_ref, *refs):
    brp = refs[0:4]
    brs = refs[4:8]
    wg_ref, wb_ref, bg_ref, op_ref, os_ref, accp, accs = refs[8:]
    i = pl.program_id(0)
    n = pl.program_id(2)

    def term(h_ref, br_ref, cols):
        wg = wg_ref[:, cols].astype(BF16)
        wb = wb_ref[:, cols].astype(BF16)
        gate = jax.nn.sigmoid(jnp.dot(h_ref[...], wg, preferred_element_type=F32) + bg_ref[0][:, cols])
        return gate * jnp.dot(br_ref[...], wb, preferred_element_type=F32)

    for b in range(N_BRANCH):
        @pl.when(n == b)
        def _(b=b):
            for c0 in range(0, op_ref.shape[1], MXU_COLS):
                cols = slice(c0, c0 + MXU_COLS)
                t = term(hp_ref, brp[b], cols)
                if b == 0:
                    accp[:, cols] = t
                elif b < N_BRANCH - 1:
                    accp[:, cols] += t
                else:
                    op_ref[:, cols] = (accp[:, cols] + t).astype(op_ref.dtype)

        @pl.when((n == b) & (i == 0))
        def _(b=b):
            t = term(hs_ref, brs[b], slice(None))
            if b == 0:
                accs[...] = t
            elif b < N_BRANCH - 1:
                accs[...] += t
            else:
                os_ref[0] = (accs[...] + t).astype(os_ref.dtype)


def gate_merge(hp, hs, branches_p, branches_s, w_gate, b_gate, w_branch, layer, *, tm=2048, tn=256):
    Mp, D = hp.shape
    Ms = hs.shape[0]
    BW = branches_p[0].shape[1]
    tm = min(tm, Mp)
    nj = D // tn
    assert Mp % tm == 0 and D % tn == 0
    bg3 = b_gate.reshape(N_BRANCH, 1, D)
    in_specs = [_resident((tm, D), lambda i, j, n: (i, 0)),
                pl.BlockSpec((Ms, D), lambda i, j, n: (0, 0))]
    in_specs += [_resident((tm, BW), lambda i, j, n: (i, 0)) for _ in range(N_BRANCH)]
    in_specs += [pl.BlockSpec((Ms, BW), lambda i, j, n: (0, 0)) for _ in range(N_BRANCH)]
    in_specs += [pl.BlockSpec((None, D, tn), lambda i, j, n: (layer, 0, n * nj + j)),
                 pl.BlockSpec((None, None, BW, tn), lambda i, j, n: (layer, n, 0, j)),
                 pl.BlockSpec((1, 1, tn), lambda i, j, n: (n, 0, j))]
    o_p, o_s = pl.pallas_call(
        _gate_merge_kernel,
        grid=(Mp // tm, nj, N_BRANCH),
        in_specs=in_specs,
        out_specs=[pl.BlockSpec((tm, tn), lambda i, j, n: (i, j)),
                   pl.BlockSpec((1, Ms, tn), lambda i, j, n: (jnp.where(i == 0, j, nj - 1), 0, 0))],
        out_shape=[jax.ShapeDtypeStruct((Mp, D), BF16), jax.ShapeDtypeStruct((nj, Ms, tn), BF16)],
        scratch_shapes=[pltpu.VMEM((tm, tn), F32), pltpu.VMEM((Ms, tn), F32)],
        compiler_params=_params("arbitrary", "arbitrary", "arbitrary"),
        name="gate_merge",
    )(hp, hs, *branches_p, *branches_s, w_gate, w_branch, bg3)
    return o_p, _unblock(o_s)


def _ffn_in_kernel(hp_ref, hs_ref, wa_ref, wb_ref, wo_ref, op_ref, os_ref, wob_ref):
    wob_ref[...] = wo_ref[...].astype(BF16)
    wa = wa_ref[...].astype(BF16)
    wb = wb_ref[...].astype(BF16)

    def act(h_ref):
        a = jnp.dot(h_ref[...], wa, preferred_element_type=F32)
        b = jnp.dot(h_ref[...], wb, preferred_element_type=F32)
        return a * jax.nn.sigmoid(a) * b

    op_ref[...] = act(hp_ref).astype(op_ref.dtype)

    @pl.when(pl.program_id(0) == 0)
    def _():
        os_ref[0] = act(hs_ref).astype(os_ref.dtype)


def ffn_in(hp, hs, w, w_out, layer, *, tm=2048, tn=256):
    Mp, D = hp.shape
    Ms = hs.shape[0]
    F = w.shape[2] // 2
    Dout = w_out.shape[2]
    tm = min(tm, Mp)
    ni, nj = Mp // tm, F // tn
    slab = F // (ni * nj)
    assert Mp % tm == 0 and F % tn == 0 and F % (ni * nj) == 0 and slab % 16 == 0
    o_p, o_s, w_out_b = pl.pallas_call(
        _ffn_in_kernel,
        grid=(ni, nj),
        in_specs=[_resident((tm, D), lambda i, j: (i, 0)),
                  pl.BlockSpec((Ms, D), lambda i, j: (0, 0)),
                  pl.BlockSpec((None, D, tn), lambda i, j: (layer, 0, j)),
                  pl.BlockSpec((None, D, tn), lambda i, j: (layer, 0, nj + j)),
                  pl.BlockSpec((None, slab, Dout), lambda i, j: (layer, i * nj + j, 0))],
        out_specs=[pl.BlockSpec((tm, tn), lambda i, j: (i, j)),
                   pl.BlockSpec((1, Ms, tn), lambda i, j: (jnp.where(i == 0, j, nj - 1), 0, 0)),
                   pl.BlockSpec((slab, Dout), lambda i, j: (i * nj + j, 0))],
        out_shape=[jax.ShapeDtypeStruct((Mp, F), BF16), jax.ShapeDtypeStruct((nj, Ms, tn), BF16),
                   jax.ShapeDtypeStruct((F, Dout), BF16)],
        compiler_params=_params("arbitrary", "arbitrary"),
        name="ffn_in",
    )(hp, hs, w, w, w_out)
    return o_p, _unblock(o_s), w_out_b


def _ada_kernel(c_ref, w_ref, b_ref, o_ref):
    c = c_ref[...]
    x = c * jax.nn.sigmoid(c)
    o_ref[0] = _bdot(x, w_ref[0]) + b_ref[0]


def ada_mod(c_all, w_ada, b_ada):
    depth, D, N = w_ada.shape
    R = c_all.shape[0]
    tn = 512
    return pl.pallas_call(
        _ada_kernel,
        grid=(depth, N // tn),
        in_specs=[pl.BlockSpec((R, D), lambda l, j: (0, 0)),
                  pl.BlockSpec((1, D, tn), lambda l, j: (l, 0, j)),
                  pl.BlockSpec((1, 1, tn), lambda l, j: (l, 0, j))],
        out_specs=pl.BlockSpec((1, R, tn), lambda l, j: (l, 0, j)),
        out_shape=jax.ShapeDtypeStruct((depth, R, N), F32),
        compiler_params=_params("parallel", "parallel"),
        name="ada_mod",
    )(c_all, w_ada, b_ada.reshape(depth, 1, N))


def _rms(x, g):
    return x * lax.rsqrt(jnp.mean(x * x, axis=-1, keepdims=True) + EPS) * g


def _norm_mod_kernel(x_ref, g_ref, sc_ref, sh_ref, h_ref):
    h = _rms(x_ref[0], g_ref[...]) * (1.0 + sc_ref[0]) + sh_ref[0]
    h_ref[0] = h.astype(h_ref.dtype)


def _mod_spec(D, row0, slot):
    return pl.BlockSpec((1, 1, D), lambda b, t: ((row0 + b) * N_MOD + slot, 0, 0))


def norm_mod(x, g, mod, row0, slots):
    B, L, D = x.shape
    tl = _pick(L, (256, 128, 64, 32, 16, 8))
    return pl.pallas_call(
        _norm_mod_kernel,
        grid=(B, L // tl),
        in_specs=[pl.BlockSpec((1, tl, D), lambda b, t: (b, t, 0)),
                  pl.BlockSpec((1, D), lambda b, t: (0, 0)),
                  _mod_spec(D, row0, slots[0]), _mod_spec(D, row0, slots[1])],
        out_specs=pl.BlockSpec((1, tl, D), lambda b, t: (b, t, 0)),
        out_shape=jax.ShapeDtypeStruct((B, L, D), BF16),
        compiler_params=_params("parallel", "parallel"),
        name="norm_mod",
    )(x, g.reshape(1, D), mod, mod)


def _resid_kernel(x_ref, y_ref, ga_ref, gate_ref, *rest, emit_h):
    xn = x_ref[0] + gate_ref[0] * _rms(y_ref[0].astype(F32), ga_ref[...])
    if emit_h:
        gb_ref, sc_ref, sh_ref, xo_ref, h_ref = rest
        xo_ref[0] = xn
        h_ref[0] = (_rms(xn, gb_ref[...]) * (1.0 + sc_ref[0]) + sh_ref[0]).astype(h_ref.dtype)
    else:
        (xo_ref,) = rest
        xo_ref[0] = xn


def resid_norm(x, y, g_a, mod, row0, gate_slot, nxt=None):
    B, L, D = x.shape
    tl = _pick(L, (256, 128, 64, 32, 16, 8))
    xspec = pl.BlockSpec((1, tl, D), lambda b, t: (b, t, 0))
    gspec = pl.BlockSpec((1, D), lambda b, t: (0, 0))
    in_specs = [xspec, xspec, gspec, _mod_spec(D, row0, gate_slot)]
    args = [x, y, g_a.reshape(1, D), mod]
    out_specs = [xspec]
    out_shape = [jax.ShapeDtypeStruct((B, L, D), F32)]
    if nxt is not None:
        g_b, mod_n, sc_slot, sh_slot = nxt
        in_specs += [gspec, _mod_spec(D, row0, sc_slot), _mod_spec(D, row0, sh_slot)]
        args += [g_b.reshape(1, D), mod_n, mod_n]
        out_specs.append(xspec)
        out_shape.append(jax.ShapeDtypeStruct((B, L, D), BF16))
    res = pl.pallas_call(
        functools.partial(_resid_kernel, emit_h=nxt is not None),
        grid=(B, L // tl),
        in_specs=in_specs, out_specs=out_specs, out_shape=out_shape,
        compiler_params=_params("parallel", "parallel"),
        name="resid_norm",
    )(*args)
    return (res[0], res[1]) if nxt is not None else (res[0], None)


def _ret_kernel(q_ref, k_ref, v_ref, g_ref, cos_ref, sin_ref, dec_ref, qd_ref, kd_ref, gl_ref, s0_ref, gnw_ref,
                o_ref, so_ref, s_scr, *, nc, dh):
    c = pl.program_id(1)

    @pl.when(c == 0)
    def _():
        s_scr[...] = s0_ref[0]

    half = dh // 2
    cos = cos_ref[...]
    sin = sin_ref[...]

    def rot(x):
        x1 = x[:, :half]
        x2 = x[:, half:]
        return jnp.concatenate([x1 * cos - x2 * sin, x1 * sin + x2 * cos], axis=-1)

    for h in range(s_scr.shape[0]):
        cols = slice(h * dh, (h + 1) * dh)
        q = rot(q_ref[0, :, cols])
        k = rot(k_ref[0, :, cols]) * (dh ** -0.5)
        v = v_ref[0, :, cols]
        S = s_scr[h]
        scores = _bdot_nt(q, k) * dec_ref[h]
        o = _bdot(scores, v) + _bdot(q * qd_ref[h], S)
        s_scr[h] = gl_ref[h, 0:1, :] * S + _bdot_tn(k * kd_ref[h], v)
        oc = o - jnp.mean(o, axis=-1, keepdims=True)
        on = oc * lax.rsqrt(jnp.mean(oc * oc, axis=-1, keepdims=True) + GN_EPS)
        g = g_ref[0, :, cols]
        o_ref[0, :, cols] = (on * gnw_ref[:, cols] * (g * jax.nn.sigmoid(g))).astype(o_ref.dtype)

    @pl.when(c == nc - 1)
    def _():
        so_ref[0] = s_scr[...]


def retention(proj, pos, state0, gn_w):
    B, L, _ = proj.shape
    H = RET_HEADS
    BW = gn_w.shape[0]
    dh = BW // H
    half = dh // 2
    C = RET_CHUNK if L % RET_CHUNK == 0 else L
    nc = L // C
    inv = jnp.exp(-math.log(ROPE_BASE) * jnp.arange(half, dtype=F32) / half)
    ang = pos.astype(F32)[:, None] * inv[None, :]
    cos, sin = jnp.cos(ang), jnp.sin(ang)
    log_g = jnp.log1p(-jnp.exp2(-5.0 - jnp.arange(H, dtype=F32)))
    idx = jnp.arange(C, dtype=F32)
    diff = idx[:, None] - idx[None, :]
    causal = diff >= 0
    dec = jnp.where(causal[None], jnp.exp(log_g[:, None, None] * jnp.where(causal, diff, 0.0)[None]), 0.0)
    qd = jnp.broadcast_to(jnp.exp(log_g[:, None] * (idx[None, :] + 1.0))[:, :, None], (H, C, dh))
    kd = jnp.broadcast_to(jnp.exp(log_g[:, None] * ((C - 1.0) - idx[None, :]))[:, :, None], (H, C, dh))
    gl = jnp.broadcast_to(jnp.exp(log_g * C)[:, None, None], (H, 8, dh))

    def slot(s):
        return pl.BlockSpec((1, C, BW), lambda b, c: (b, c, s))

    tab = pl.BlockSpec((C, half), lambda b, c: (c, 0))
    const = lambda a: pl.BlockSpec(a.shape, lambda b, c: (0,) * a.ndim)
    return pl.pallas_call(
        functools.partial(_ret_kernel, nc=nc, dh=dh),
        grid=(B, nc),
        in_specs=[slot(0), slot(1), slot(2), slot(3), tab, tab, const(dec), const(qd), const(kd), const(gl),
                  pl.BlockSpec((1, H, dh, dh), lambda b, c: (b, 0, 0, 0)),
                  pl.BlockSpec((1, BW), lambda b, c: (0, 0))],
        out_specs=[pl.BlockSpec((1, C, BW), lambda b, c: (b, c, 0)),
                   pl.BlockSpec((1, H, dh, dh), lambda b, c: (b, 0, 0, 0))],
        out_shape=[jax.ShapeDtypeStruct((B, L, BW), BF16),
                   jax.ShapeDtypeStruct((B, H, dh, dh), F32)],
        scratch_shapes=[pltpu.VMEM((H, dh, dh), F32)],
        compiler_params=_params("parallel", "arbitrary"),
        name="retention",
    )(proj, proj, proj, proj, cos, sin, dec, qd, kd, gl, state0, gn_w.reshape(1, BW))


def t5_bucket(n):
    max_exact = REL_BUCKETS // 2
    nf = jnp.maximum(n, 1).astype(F32)
    large = max_exact + (jnp.log(nf / max_exact) / math.log(REL_MAX_DIST / max_exact) * (REL_BUCKETS - max_exact)).astype(jnp.int32)
    large = jnp.minimum(large, REL_BUCKETS - 1)
    return jnp.where(n < max_exact, n, large)


def _bias_lookup(dist, rb_rows):
    k = jnp.arange(REL_BUCKETS)
    return jnp.sum(jnp.where(t5_bucket(dist)[..., None] == k, rb_rows, 0.0), axis=-1)


def _select_blocks(gate, n_elig, nb):
    col = lax.broadcasted_iota(jnp.int32, gate.shape, 1)
    rank = jnp.zeros(gate.shape, jnp.int32)
    for m in range(nb):
        gm = gate[:, m:m + 1]
        beats = (gm > gate) | ((gm == gate) & (m < col))
        rank = rank + jnp.where(beats, jnp.where(m < n_elig, 1, 0), 0)
    return (col < n_elig) & (rank < MOBA_TOPK)


def _moba_prompt_kernel(q_ref, k_ref, v_ref, tb_ref, o_ref, kb_scr, vb_scr, *, nb, blk, scale):
    L = nb * blk
    kb_scr[...] = k_ref[0].astype(BF16)
    vb_scr[...] = v_ref[0].astype(BF16)
    kbar = jnp.concatenate(
        [jnp.mean(k_ref[0, j * blk:(j + 1) * blk, :], axis=0, keepdims=True) for j in range(nb)], axis=0)
    q = q_ref[0]
    gate = _fdot_nt(kbar, q)
    n_idx = lax.broadcasted_iota(jnp.int32, (nb, L), 0)
    own = lax.broadcasted_iota(jnp.int32, (nb, L), 1) // blk
    rank = jnp.zeros((nb, L), jnp.int32)
    for m in range(nb):
        gm = gate[m:m + 1, :]
        beats = (gm > gate) | ((gm == gate) & (m < n_idx))
        rank = rank + jnp.where(beats & (m < own), 1, 0)
    sel = jnp.where((n_idx < own) & (rank < MOBA_TOPK), 1.0, 0.0).T

    r = lax.broadcasted_iota(jnp.int32, (blk, blk), 0)
    c = lax.broadcasted_iota(jnp.int32, (blk, blk), 1)
    causal = c <= r
    far_bias = tb_ref[0, 0:1, 0:1]
    for i in range(nb):
        rows = slice(i * blk, (i + 1) * blk)
        s = _bdot_nt(q[rows] * scale, kb_scr[0:(i + 1) * blk, :])
        parts = []
        for j in range(i + 1):
            sj = s[:, j * blk:(j + 1) * blk]
            if j == i:
                parts.append(jnp.where(causal, sj + tb_ref[0, :, blk:], -jnp.inf))
            else:
                bias = tb_ref[0, :, :blk] if j == i - 1 else far_bias
                parts.append(jnp.where(sel[rows, j:j + 1] > 0.0, sj + bias, -jnp.inf))
        s = jnp.concatenate(parts, axis=1) if len(parts) > 1 else parts[0]
        m = jnp.max(s, axis=-1, keepdims=True)
        p = jnp.exp(s - m)
        l = jnp.sum(p, axis=-1, keepdims=True)
        o_ref[0, rows, :] = (_bdot(p, vb_scr[0:(i + 1) * blk, :]) / l).astype(o_ref.dtype)


def moba_prompt(proj, rel_bias, H):
    B, L, _ = proj.shape
    blk = MOBA_BLOCK
    dh = MOBA_HEAD_DIM
    nb = L // blk
    assert L % blk == 0 and blk > REL_MAX_DIST
    P = 2 * blk + 1
    mm = jnp.arange(P)
    dvec = jnp.where(mm <= blk, blk - mm, 2 * blk - 1)
    bvec = _bias_lookup(dvec[None, :], rel_bias.astype(F32).T[:, None, :])
    tb = jnp.tile(bvec, (1, blk))[:, :blk * (P - 1)].reshape(H, blk, P - 1)
    return pl.pallas_call(
        functools.partial(_moba_prompt_kernel, nb=nb, blk=blk, scale=dh ** -0.5),
        grid=(B, H),
        in_specs=[pl.BlockSpec((1, L, dh), lambda b, h: (b, 0, 4 * H + h)),
                  pl.BlockSpec((1, L, dh), lambda b, h: (b, 0, 5 * H + h)),
                  pl.BlockSpec((1, L, dh), lambda b, h: (b, 0, 6 * H + h)),
                  pl.BlockSpec((1, blk, 2 * blk), lambda b, h: (h, 0, 0))],
        out_specs=pl.BlockSpec((1, L, dh), lambda b, h: (b, 0, h)),
        out_shape=jax.ShapeDtypeStruct((B, L, H * dh), BF16),
        scratch_shapes=[pltpu.VMEM((L, dh), BF16), pltpu.VMEM((L, dh), BF16)],
        compiler_params=_params("parallel", "parallel"),
        name="moba_prompt",
    )(proj, proj, proj, tb)


def _moba_sample_kernel(pt_ref, q_ref, kn_ref, vn_ref, bown_ref, blast_ref, bfar_ref, *refs,
                        pps, ppb, ns, nbp, H, scale, inv_blk):
    k_refs = refs[:pps]
    v_refs = refs[pps:2 * pps]
    o_ref = refs[2 * pps]
    q_scr, p_scr, gate_scr, ksum_scr, l_scr, acc_scr = refs[2 * pps + 1:]
    s = pl.program_id(1)
    ds = q_ref.shape[1]
    dh = q_scr.shape[1]
    R = H * ds
    page = k_refs[0].shape[2] // H
    n_pages = ns * pps

    def head_major(ref):
        return jnp.concatenate([ref[0, :, h * dh:(h + 1) * dh] for h in range(H)], axis=0)

    @pl.when(s == 0)
    def _():
        q_scr[...] = head_major(q_ref)
        gate_scr[...] = jnp.zeros_like(gate_scr)

    @pl.when(s < ns)
    def _():
        lane = lax.broadcasted_iota(jnp.int32, (ds, LANES), 1)
        for i in range(pps):
            pg = s * pps + i
            for h in range(H):
                rows = slice(h * ds, (h + 1) * ds)
                kh = k_refs[i][0, 0, pl.ds(h, page, stride=H), :]
                p_scr[pg, rows, :] = _bdot_nt(q_scr[rows, :], kh) * scale
                tot = jnp.sum(kh, axis=0, keepdims=True)
                if i % ppb == 0:
                    ksum_scr[h:h + 1, :] = tot
                else:
                    ksum_scr[h:h + 1, :] += tot
                if i % ppb == ppb - 1:
                    g = jnp.sum(q_scr[rows, :] * (ksum_scr[h:h + 1, :] * inv_blk), axis=-1, keepdims=True)
                    gate_scr[rows, :] += jnp.where(lane == pg // ppb, g, 0.0)

    @pl.when(s == ns)
    def _():
        sel = jnp.where(_select_blocks(gate_scr[...], nbp, nbp), 1.0, 0.0)
        far = bfar_ref[:, 0:1]
        q = q_scr[...]
        kn = head_major(kn_ref)
        vn = head_major(vn_ref)
        so = _bdot_nt(q, kn) * scale + bown_ref[...]
        r = lax.broadcasted_iota(jnp.int32, (R, R), 0)
        c = lax.broadcasted_iota(jnp.int32, (R, R), 1)
        so = jnp.where((c // ds == r // ds) & (c % ds <= r % ds), so, -jnp.inf)
        m_own = jnp.max(so, axis=-1, keepdims=True)

        picked = [jnp.broadcast_to(sel[:, n:n + 1] > 0.0, (R, page)) for n in range(nbp)]

        def masked(pg):
            return jnp.where(picked[pg // ppb], p_scr[pg], -jnp.inf)

        mx = masked(0)
        for pg in range(1, n_pages - 1):
            mx = jnp.maximum(mx, masked(pg))
        last = masked(n_pages - 1) + blast_ref[...]
        m = jnp.maximum(jnp.max(jnp.maximum(mx + far, last), axis=-1, keepdims=True), m_own)
        shift = far - m
        lsum = jnp.exp(last - m)
        p_scr[n_pages - 1] = lsum
        for pg in range(n_pages - 1):
            p = jnp.exp(masked(pg) + shift)
            p_scr[pg] = p
            lsum = lsum + p
        p_own = jnp.exp(so - m)
        l_scr[...] = jnp.sum(lsum, axis=-1, keepdims=True) + jnp.sum(p_own, axis=-1, keepdims=True)
        acc_scr[...] = _bdot(p_own, vn)

    @pl.when(s >= ns)
    def _():
        for i in range(pps):
            pg = (s - ns) * pps + i
            for h in range(H):
                rows = slice(h * ds, (h + 1) * ds)
                vh = v_refs[i][0, 0, pl.ds(h, page, stride=H), :]
                acc_scr[rows, :] += _bdot(p_scr[pg, rows, :], vh)

        @pl.when(s == 2 * ns - 1)
        def _():
            o = acc_scr[...] / l_scr[...]
            for h in range(H):
                o_ref[0, :, h * dh:(h + 1) * dh] = o[h * ds:(h + 1) * ds, :].astype(o_ref.dtype)


def moba_sample(proj, cache_k, cache_v, layer, page_table, rel_bias):
    DB, DS, _ = proj.shape
    _, n_pool, page, H, dh = cache_k.shape
    n_pages = page_table.shape[1]
    past_len = n_pages * page
    ppb = MOBA_BLOCK // page
    pps = min(KV_PAGES_PER_STEP, n_pages)
    nbp = past_len // MOBA_BLOCK
    assert past_len % MOBA_BLOCK == 0 and DS <= MOBA_BLOCK and page >= REL_MAX_DIST and DS <= page
    assert MOBA_BLOCK % page == 0 and n_pages % pps == 0 and pps % ppb == 0 and nbp <= LANES and page == LANES
    ns = n_pages // pps
    R = H * DS
    BW = H * dh
    rb_rows = jnp.repeat(rel_bias.astype(F32).T, DS, axis=0)[:, None, :]
    hq_q = jnp.tile(jnp.arange(DS), H)
    b_own = _bias_lookup(jnp.maximum(hq_q[:, None] - hq_q[None, :], 0), rb_rows)
    b_last = _bias_lookup(page + hq_q[:, None] - jnp.arange(page)[None, :], rb_rows)
    b_far = jnp.broadcast_to(rb_rows[:, :, REL_BUCKETS - 1], (R, LANES))
    ck = cache_k.reshape(cache_k.shape[0], n_pool, page * H, dh)
    cv = cache_v.reshape(cache_v.shape[0], n_pool, page * H, dh)

    def kspec(i):
        return pl.BlockSpec((1, 1, page * H, dh),
                            lambda b, s, pt: (layer, pt[b, jnp.minimum(s, ns - 1) * pps + i], 0, 0))

    def vspec(i):
        return pl.BlockSpec((1, 1, page * H, dh),
                            lambda b, s, pt: (layer, pt[b, jnp.maximum(s - ns, 0) * pps + i], 0, 0))

    def slot(sl):
        return pl.BlockSpec((1, DS, BW), lambda b, s, pt: (b, 0, sl))

    const = lambda a: pl.BlockSpec(a.shape, lambda b, s, pt: (0,) * a.ndim)
    return pl.pallas_call(
        functools.partial(_moba_sample_kernel, pps=pps, ppb=ppb, ns=ns, nbp=nbp, H=H, scale=dh ** -0.5,
                          inv_blk=1.0 / MOBA_BLOCK),
        grid_spec=pltpu.PrefetchScalarGridSpec(
            num_scalar_prefetch=1,
            grid=(DB, 2 * ns),
            in_specs=[slot(4), slot(5), slot(6), const(b_own), const(b_last), const(b_far)]
                     + [kspec(i) for i in range(pps)] + [vspec(i) for i in range(pps)],
            out_specs=pl.BlockSpec((1, DS, BW), lambda b, s, pt: (b, 0, 0)),
            scratch_shapes=[pltpu.VMEM((R, dh), F32), pltpu.VMEM((n_pages, R, page), F32),
                            pltpu.VMEM((R, LANES), F32), pltpu.VMEM((H, dh), F32),
                            pltpu.VMEM((R, 1), F32), pltpu.VMEM((R, dh), F32)]),
        out_shape=jax.ShapeDtypeStruct((DB, DS, BW), BF16),
        compiler_params=_params("parallel", "arbitrary"),
        name="moba_sample",
    )(page_table, proj, proj, proj, b_own, b_last, b_far, *([ck] * pps), *([cv] * pps))


def _conv_kernel(p7, p8, p9, p10, p11, scb, cfb, sw, cw, cb, nw, nbias, oc_ref, od_ref, scn_ref, cfn_ref,
                 extc, extd, ybuf, win, *, tl, nt, ws, wd):
    t = pl.program_id(1)
    P = CONV_PAD
    W = ybuf.shape[1]

    @pl.when(t == 0)
    def _():
        extc[pl.ds(P - (ws - 1), ws - 1), :] = scb[0]
        extd[pl.ds(P - (wd - 1), wd - 1), :] = cfb[0]

    extc[pl.ds(P, tl), :] = p8[0] * p9[0]
    extd[pl.ds(P, tl), :] = p10[0] * jax.nn.sigmoid(p11[0])

    yc = sw[0:1, :] * extc[pl.ds(P - (ws - 1), tl), :]
    for j in range(1, ws):
        yc = yc + sw[j:j + 1, :] * extc[pl.ds(P - (ws - 1) + j, tl), :]
    oc_ref[0] = (p7[0] * yc).astype(oc_ref.dtype)

    phases = min(SUBLANES, wd)
    for a in range(phases):
        rows = tl + (wd - 1 - a) // SUBLANES * SUBLANES
        win[a, pl.ds(0, rows), :] = extd[pl.ds(P - (wd - 1) + a, rows), :]
    for c0 in range(0, W, LANES):
        cols = slice(c0, c0 + LANES)
        acc = None
        for j in range(wd):
            a = j % phases
            term = cw[j:j + 1, cols] * win[a, pl.ds(j - a, tl), cols]
            acc = term if acc is None else acc + term
        ybuf[:, cols] = acc + cb[:, cols]
    y = ybuf[...]
    yc_ = y - jnp.mean(y, axis=-1, keepdims=True)
    yn = yc_ * lax.rsqrt(jnp.mean(yc_ * yc_, axis=-1, keepdims=True) + EPS) * nw[...] + nbias[...]
    od_ref[0] = (yn * jax.nn.sigmoid(yn)).astype(od_ref.dtype)

    newc = extc[pl.ds(P + tl - (ws - 1), ws - 1), :]
    newd = extd[pl.ds(P + tl - (wd - 1), wd - 1), :]
    extc[pl.ds(P - (ws - 1), ws - 1), :] = newc
    extd[pl.ds(P - (wd - 1), wd - 1), :] = newd

    @pl.when(t == nt - 1)
    def _():
        scn_ref[0] = newc
        cfn_ref[0] = newd


def conv_branches(proj, sc_buf, cf_buf, sconv_w, cconv_w, cconv_b, cnorm_w, cnorm_b):
    B, L, _ = proj.shape
    ws, BW = sconv_w.shape
    wd = cconv_w.shape[0]
    assert wd - 1 <= CONV_PAD
    tl = _pick(L, (128, 64, 32, 16, 8))
    nt = L // tl

    def slot(s):
        return pl.BlockSpec((1, tl, BW), lambda b, t: (b, t, s))

    full = lambda a: pl.BlockSpec(a.shape, lambda b, t: (0,) * a.ndim)
    cb2, nw2, nb2 = cconv_b.reshape(1, BW), cnorm_w.reshape(1, BW), cnorm_b.reshape(1, BW)
    ospec = pl.BlockSpec((1, tl, BW), lambda b, t: (b, t, 0))
    return pl.pallas_call(
        functools.partial(_conv_kernel, tl=tl, nt=nt, ws=ws, wd=wd),
        grid=(B, nt),
        in_specs=[slot(7), slot(8), slot(9), slot(10), slot(11),
                  pl.BlockSpec((1, ws - 1, BW), lambda b, t: (b, 0, 0)),
                  pl.BlockSpec((1, wd - 1, BW), lambda b, t: (b, 0, 0)),
                  full(sconv_w), full(cconv_w), full(cb2), full(nw2), full(nb2)],
        out_specs=[ospec, ospec,
                   pl.BlockSpec((1, ws - 1, BW), lambda b, t: (b, 0, 0)),
                   pl.BlockSpec((1, wd - 1, BW), lambda b, t: (b, 0, 0))],
        out_shape=[jax.ShapeDtypeStruct((B, L, BW), BF16), jax.ShapeDtypeStruct((B, L, BW), BF16),
                   jax.ShapeDtypeStruct((B, ws - 1, BW), F32), jax.ShapeDtypeStruct((B, wd - 1, BW), F32)],
        scratch_shapes=[pltpu.VMEM((CONV_PAD + tl, BW), F32), pltpu.VMEM((CONV_PAD + tl, BW), F32),
                        pltpu.VMEM((tl, BW), F32),
                        pltpu.VMEM((SUBLANES, tl + CONV_PAD - SUBLANES, BW), F32)],
        compiler_params=_params("parallel", "arbitrary"),
        name="conv_branches",
    )(proj, proj, proj, proj, proj, sc_buf, cf_buf, sconv_w, cconv_w, cb2, nw2, nb2)


def _kv_out_kernel(*refs, depth, H):
    k_refs = refs[0:depth]
    v_refs = refs[depth:2 * depth]
    ok_ref, ov_ref = refs[2 * depth:]
    l_id = pl.program_id(0)
    tl = k_refs[0].shape[1]
    dh = ok_ref.shape[3]
    for l in range(depth):
        @pl.when(l_id == l)
        def _(l=l):
            for h in range(H):
                ok_ref[0, 0, pl.ds(h, tl, stride=H), :] = k_refs[l][0, :, h * dh:(h + 1) * dh]
                ov_ref[0, 0, pl.ds(h, tl, stride=H), :] = v_refs[l][0, :, h * dh:(h + 1) * dh]


def kv_outputs(projs, H):
    depth = len(projs)
    B, L, W = projs[0].shape
    BW = W // N_IN_SLOTS
    dh = BW // H
    tl = _pick(L, (128, 64, 32, 16, 8))
    nt = L // tl

    def spec(l, slot_id):
        return pl.BlockSpec((1, tl, BW), lambda li, b, t: (
            jnp.where(li == l, b, jnp.where(li < l, 0, B - 1)),
            jnp.where(li == l, t, jnp.where(li < l, 0, nt - 1)), slot_id))

    ospec = pl.BlockSpec((1, 1, tl * H, dh), lambda li, b, t: (li, b, t, 0))
    return pl.pallas_call(
        functools.partial(_kv_out_kernel, depth=depth, H=H),
        grid=(depth, B, nt),
        in_specs=[spec(l, 5) for l in range(depth)] + [spec(l, 6) for l in range(depth)],
        out_specs=[ospec, ospec],
        out_shape=[jax.ShapeDtypeStruct((depth, B, L * H, dh), F32)] * 2,
        compiler_params=_params("arbitrary", "arbitrary", "arbitrary"),
        name="kv_outputs",
    )(*projs, *projs)


def _branches(proj, pos, past, layer, ret_state, sc_buf, cf_buf, lw):
    (ret_gn_w, sconv_w, cconv_w, cconv_b, cnorm_w, cnorm_b, rel_bias) = lw
    B, L, W = proj.shape
    BW = W // N_IN_SLOTS
    H = BW // MOBA_HEAD_DIM
    o_a, ret_new = retention(proj, pos, ret_state, ret_gn_w)
    if past is None:
        o_b = moba_prompt(proj, rel_bias, H)
    else:
        cache_k, cache_v, page_table = past
        o_b = moba_sample(proj, cache_k, cache_v, layer, page_table, rel_bias)
    o_c, o_d, sc_new, cf_new = conv_branches(proj, sc_buf, cf_buf, sconv_w, cconv_w, cconv_b, cnorm_w, cnorm_b)
    k_b = proj[:, :, 5 * BW:6 * BW].reshape(B, L, H, MOBA_HEAD_DIM)
    v_b = proj[:, :, 6 * BW:7 * BW].reshape(B, L, H, MOBA_HEAD_DIM)
    return [a.reshape(B * L, BW) for a in (o_a, o_b, o_c, o_d)], (k_b, v_b, ret_new, sc_new, cf_new)


def kernel(x_prompt, x_sample, c_prompt, c_sample, cache_k, cache_v, state_ret, state_sconv, state_cconv, page_table, w_ada, b_ada, norm_g, w_in, w_gate, b_gate, ret_gn_w, sconv_w, cconv_w, cconv_b, cnorm_w, cnorm_b, w_branch, w_o, w_ffn_in, w_ffn_out, rel_bias):
    B, S, D = x_prompt.shape
    DB, DS, _ = x_sample.shape
    depth = w_in.shape[0]
    BW = D // N_BRANCH
    H = BW // MOBA_HEAD_DIM
    dh_ret = BW // RET_HEADS
    F = w_ffn_out.shape[1]
    page = cache_k.shape[2]
    past_len = page_table.shape[1] * page
    pos_p = jnp.arange(S, dtype=jnp.int32)
    pos_s = past_len + jnp.arange(DS, dtype=jnp.int32)
    Mp, Ms = B * S, DB * DS

    rows = B + DB
    rows_pad = -(-rows // 8) * 8
    c_all = jnp.concatenate([c_prompt, c_sample, jnp.zeros((rows_pad - rows, D), F32)], axis=0)
    mods = ada_mod(c_all, w_ada, b_ada).reshape(depth, rows_pad * N_MOD, 1, D)

    xp, xs = x_prompt, x_sample
    hp = norm_mod(xp, norm_g[0, 0], mods[0], 0, (1, 0))
    hs = norm_mod(xs, norm_g[0, 0], mods[0], B, (1, 0))
    zeros_ret = jnp.zeros((B, RET_HEADS, dh_ret, dh_ret), F32)
    zeros_sc = jnp.zeros((B, sconv_w.shape[1] - 1, BW), F32)
    zeros_cf = jnp.zeros((B, cconv_w.shape[1] - 1, BW), F32)
    outs_p, outs_s, projs_p = [], [], []
    for l in range(depth):
        mod = mods[l]
        lw = (ret_gn_w[l], sconv_w[l], cconv_w[l], cconv_b[l], cnorm_w[l], cnorm_b[l], rel_bias)
        nxt = (norm_g[l + 1, 0], mods[l + 1], 1, 0) if l + 1 < depth else None
        hp2, hs2 = hp.reshape(Mp, D), hs.reshape(Ms, D)
        proj_p, proj_s = matmul(hp2, hs2, w_in, l, tm=2048, tn=512)
        br_p, op = _branches(proj_p.reshape(B, S, -1), pos_p, None, l, zeros_ret, zeros_sc, zeros_cf, lw)
        br_s, os_ = _branches(proj_s.reshape(DB, DS, -1), pos_s, (cache_k, cache_v, page_table), l,
                              state_ret[l], state_sconv[l], state_cconv[l], lw)
        mg_p, mg_s = gate_merge(hp2, hs2, br_p, br_s, w_gate, b_gate[l], w_branch, l)
        y_p, y_s = matmul(mg_p, mg_s, w_o, l, tm=2048, tn=512, out_dtype=BF16)
        ffn_mod = (norm_g[l, 2], mod, 4, 3)
        xp, hp = resid_norm(xp, y_p.reshape(B, S, D), norm_g[l, 1], mod, 0, 2, nxt=ffn_mod)
        xs, hs = resid_norm(xs, y_s.reshape(DB, DS, D), norm_g[l, 1], mod, B, 2, nxt=ffn_mod)
        act_p, act_s, w_out_b = ffn_in(hp.reshape(Mp, D), hs.reshape(Ms, D), w_ffn_in, w_ffn_out, l)
        y_p, y_s = matmul(act_p, act_s, w_out_b[None], 0, tm=1024, tn=512, tk=F // 2, out_dtype=BF16)
        xp, hp = resid_norm(xp, y_p.reshape(B, S, D), norm_g[l, 3], mod, 0, 5, nxt=nxt)
        xs, hs = resid_norm(xs, y_s.reshape(DB, DS, D), norm_g[l, 3], mod, B, 5, nxt=nxt)
        outs_p.append(op)
        outs_s.append(os_)
        projs_p.append(proj_p.reshape(B, S, -1))
    stack = lambda outs, i: jnp.stack([o[i] for o in outs])
    new_k_prompt, new_v_prompt = (a.reshape(depth, B, S // page, page, H, MOBA_HEAD_DIM)
                                  for a in kv_outputs(projs_p, H))
    return (xp, xs, new_k_prompt, new_v_prompt, stack(outs_s, 0), stack(outs_s, 1),
            stack(outs_p, 2), stack(outs_s, 2), stack(outs_p, 3), stack(outs_s, 3),
            stack(outs_p, 4), stack(outs_s, 4))
```
